```python
import math
import jax, jax.numpy as jnp
from jax import lax
import numpy as np

D_MODEL = 1024
BATCH = 16
SEQ = 2048
DEPTH = 2
DEC_BATCH = 128
DEC_SEQ = 1
PAST_LEN = 16384
PAGE_SIZE = 128

N_META = 16
NORM_EPS = 1e-6
SSD_HEADS = 16
SSD_HEAD_DIM = 64
SSD_INNER = SSD_HEADS * SSD_HEAD_DIM
SSD_GROUPS = 4
HEADS_PER_GROUP = SSD_HEADS // SSD_GROUPS
SSD_STATE = 128
SSD_CONV = 4
SSD_CHUNK = 128
SSD_CONV_DIM = SSD_INNER + 2 * SSD_GROUPS * SSD_STATE
MLA_HEADS = 8
MLA_NOPE = 64
MLA_ROPE = 32
MLA_V = 64
MLA_Q_LORA = 384
MLA_KV_LORA = 256
MLA_INNER = MLA_HEADS * MLA_V
MLA_SCALE = (MLA_NOPE + MLA_ROPE) ** -0.5
ROPE_BASE = 10000.0
Q_BLOCK = 128
IN_SPLITS = (SSD_INNER, SSD_CONV_DIM, SSD_HEADS, MLA_Q_LORA, MLA_KV_LORA, MLA_ROPE, MLA_INNER, 2 * D_MODEL)
IN_DIM = SSD_INNER + SSD_CONV_DIM + SSD_HEADS + MLA_Q_LORA + MLA_KV_LORA + MLA_ROPE + MLA_INNER + 2 * D_MODEL

kernel_name = 'hybrid_ssd_mla_gated_decoder_step'


def rmsnorm(x, w):
    xf = x.astype(jnp.float32)
    y = xf * lax.rsqrt(jnp.mean(xf * xf, -1, keepdims=True) + NORM_EPS)
    return (y * w.astype(jnp.float32)).astype(x.dtype)


def split_cols(a, sizes):
    out, o = [], 0
    for n in sizes:
        out.append(a[..., o:o + n])
        o += n
    return out


def rope(x, pos):
    half = x.shape[-1] // 2
    inv = 1.0 / (ROPE_BASE ** (jnp.arange(half, dtype=jnp.float32) / half))
    ang = pos.astype(jnp.float32)[:, None] * inv[None, :]
    cos = jnp.cos(ang)[None, :, None, :]
    sin = jnp.sin(ang)[None, :, None, :]
    xf = x.astype(jnp.float32)
    x1, x2 = xf[..., :half], xf[..., half:]
    return jnp.concatenate([x1 * cos - x2 * sin, x1 * sin + x2 * cos], -1).astype(x.dtype)


def in_project(x, norm_w, w_in):
    h = rmsnorm(x, norm_w)
    return split_cols(h @ w_in, IN_SPLITS)


def causal_conv(xbc, ctx, w, b):
    xp = jnp.concatenate([ctx, xbc], 1)
    T = xbc.shape[1]
    out = sum(xp[:, k:k + T] * w[k] for k in range(SSD_CONV)) + b
    return jax.nn.silu(out), xp[:, xp.shape[1] - (SSD_CONV - 1):]


def ssd_inputs(xbc, dt_raw, dt_bias):
    b, T, _ = xbc.shape
    xs, bm, cm = split_cols(xbc.astype(jnp.float32), (SSD_INNER, SSD_GROUPS * SSD_STATE, SSD_GROUPS * SSD_STATE))
    xs = xs.reshape(b, T, SSD_HEADS, SSD_HEAD_DIM)
    bm = bm.reshape(b, T, SSD_GROUPS, SSD_STATE)
    cm = cm.reshape(b, T, SSD_GROUPS, SSD_STATE)
    dt = jax.nn.softplus(dt_raw.astype(jnp.float32) + dt_bias.astype(jnp.float32))
    return xs, dt, bm, cm


def ssd_chunked(x, dt, A, B, C, s0, chunk):
    b, L = x.shape[:2]
    nc = L // chunk
    G, E, P, N = SSD_GROUPS, HEADS_PER_GROUP, SSD_HEAD_DIM, SSD_STATE
    x = x.reshape(b, nc, chunk, G, E, P)
    dt = dt.reshape(b, nc, chunk, G, E)
    B = B.reshape(b, nc, chunk, G, N)
    C = C.reshape(b, nc, chunk, G, N)
    acs = jnp.cumsum(dt * A.reshape(G, E), axis=2)
    xdt = x * dt[..., None]
    diff = acs[:, :, :, None] - acs[:, :, None, :]
    causal = jnp.tril(jnp.ones((chunk, chunk), bool))[:, :, None, None]
    decay = jnp.exp(jnp.where(causal, diff, -jnp.inf))
    cb = jnp.einsum('bcign,bcjgn->bcijg', C, B)
    y_diag = jnp.einsum('bcijg,bcijge,bcjgep->bcigep', cb, decay, xdt)
    decay_to_end = jnp.exp(acs[:, :, -1:] - acs)
    chunk_states = jnp.einsum('bclgn,bclge,bclgep->bcgepn', B, decay_to_end, xdt)
    chunk_decay = jnp.exp(acs[:, :, -1])

    def step(s, inp):
        dec, st = inp
        return dec[..., None, None] * s + st, s

    s_final, s_in = lax.scan(step, s0.reshape(b, G, E, P, N),
                             (jnp.moveaxis(chunk_decay, 1, 0), jnp.moveaxis(chunk_states, 1, 0)))
    s_in = jnp.moveaxis(s_in, 0, 1)
    y_off = jnp.einsum('bcign,bcgepn,bcige->bcigep', C, s_in, jnp.exp(acs))
    y = (y_diag + y_off).reshape(b, L, SSD_HEADS, P)
    return y, s_final.reshape(b, SSD_HEADS, P, N)


def ssd_recurrent(x, dt, A, B, C, s0):
    b = x.shape[0]
    G, E, P, N = SSD_GROUPS, HEADS_PER_GROUP, SSD_HEAD_DIM, SSD_STATE
    a = A.reshape(G, E)

    def step(s, inp):
        xt, dtt, bt, ct = inp
        xt = xt.reshape(b, G, E, P)
        dtt = dtt.reshape(b, G, E)
        s = jnp.exp(dtt * a)[..., None, None] * s + jnp.einsum('bge,bgep,bgn->bgepn', dtt, xt, bt)
        y = jnp.einsum('bgn,bgepn->bgep', ct, s)
        return s, y.reshape(b, SSD_HEADS, P)

    s_final, y = lax.scan(step, s0.reshape(b, G, E, P, N),
                          (jnp.moveaxis(x, 1, 0), jnp.moveaxis(dt, 1, 0), jnp.moveaxis(B, 1, 0), jnp.moveaxis(C, 1, 0)))
    return jnp.moveaxis(y, 0, 1), s_final.reshape(b, SSD_HEADS, P, N)


def ssd_output(y, xs, z, d_skip, norm_w):
    b, T = y.shape[:2]
    y = (y + d_skip.astype(jnp.float32)[:, None] * xs).reshape(b, T, SSD_INNER)
    return rmsnorm(y * jax.nn.silu(z.astype(jnp.float32)), norm_w).astype(z.dtype)


def mla_project(cq, ckv_raw, kr_raw, pos, q_norm_w, w_uq, kv_norm_w):
    b, T = cq.shape[:2]
    q = (rmsnorm(cq, q_norm_w) @ w_uq).reshape(b, T, MLA_HEADS, MLA_NOPE + MLA_ROPE)
    q_nope, q_rope = q[..., :MLA_NOPE], rope(q[..., MLA_NOPE:], pos)
    c_kv = rmsnorm(ckv_raw, kv_norm_w)
    k_rope = rope(kr_raw[:, :, None, :], pos)[:, :, 0]
    return q_nope, q_rope, c_kv, k_rope


def mla_prompt_attn(q_nope, q_rope, c_kv, k_rope, w_uk, w_uv):
    f32 = jnp.float32
    b, L = c_kv.shape[:2]
    ckv = c_kv.astype(f32)
    k_nope = jnp.einsum('blr,rhd->blhd', ckv, w_uk.astype(f32).reshape(MLA_KV_LORA, MLA_HEADS, MLA_NOPE))
    v = jnp.einsum('blr,rhd->blhd', ckv, w_uv.astype(f32).reshape(MLA_KV_LORA, MLA_HEADS, MLA_V))
    k = jnp.concatenate([k_nope, jnp.broadcast_to(k_rope.astype(f32)[:, :, None], (b, L, MLA_HEADS, MLA_ROPE))], -1)
    q = jnp.concatenate([q_nope, q_rope], -1).astype(f32)
    kpos = jnp.arange(L)

    def attend(qb, qpos, kk, vv, kp):
        s = jnp.einsum('bqhd,bkhd->bhqk', qb, kk) * MLA_SCALE
        s = jnp.where(kp[None, :] <= qpos[:, None], s, -jnp.inf)
        return jnp.einsum('bhqk,bkhd->bqhd', jax.nn.softmax(s, -1), vv)

    o_meta = attend(q[:, :N_META], kpos[:N_META], k[:, :N_META], v[:, :N_META], kpos[:N_META])
    n_blk = (L - N_META) // Q_BLOCK
    qb = jnp.moveaxis(q[:, N_META:].reshape(b, n_blk, Q_BLOCK, MLA_HEADS, MLA_NOPE + MLA_ROPE), 1, 0)
    starts = N_META + jnp.arange(n_blk) * Q_BLOCK
    o_real = lax.map(lambda a: attend(a[0], a[1] + jnp.arange(Q_BLOCK), k, v, kpos), (qb, starts))
    o_real = jnp.moveaxis(o_real, 0, 1).reshape(b, L - N_META, MLA_HEADS, MLA_V)
    return jnp.concatenate([o_meta, o_real], 1)


def online_update(carry, s, vals):
    m, l, acc = carry
    m_new = jnp.maximum(m, s.max(-1))
    corr = jnp.exp(m - m_new)
    pr = jnp.exp(s - m_new[..., None])
    return (m_new, l * corr + pr.sum(-1), acc * corr[..., None] + jnp.einsum('bthk,bkr->bthr', pr, vals))


def mla_sample_attn(q_nope, q_rope, c_kv, k_rope, cache_ckv, cache_kr, page_table, layer, w_uk, w_uv):
    f32 = jnp.float32
    b, T = q_nope.shape[:2]
    q_lat = jnp.einsum('bthd,rhd->bthr', q_nope.astype(f32), w_uk.astype(f32).reshape(MLA_KV_LORA, MLA_HEADS, MLA_NOPE))
    qr = q_rope.astype(f32)

    def scores(ckv, kr):
        return (jnp.einsum('bthr,bkr->bthk', q_lat, ckv) + jnp.einsum('bthd,bkd->bthk', qr, kr)) * MLA_SCALE

    def step(carry, pages):
        ckv = cache_ckv[layer, pages].astype(f32)
        kr = cache_kr[layer, pages].astype(f32)
        return online_update(carry, scores(ckv, kr), ckv), None

    init = (jnp.full((b, T, MLA_HEADS), -jnp.inf, f32), jnp.zeros((b, T, MLA_HEADS), f32),
            jnp.zeros((b, T, MLA_HEADS, MLA_KV_LORA), f32))
    carry, _ = lax.scan(step, init, page_table.T)
    ckv_new = c_kv.astype(f32)
    s_new = scores(ckv_new, k_rope.astype(f32))
    mask = jnp.tril(jnp.ones((T, T), bool))[None, :, None, :]
    m, l, acc = online_update(carry, jnp.where(mask, s_new, -jnp.inf), ckv_new)
    o_lat = acc / l[..., None]
    return jnp.einsum('bthr,rhd->bthd', o_lat, w_uv.astype(f32).reshape(MLA_KV_LORA, MLA_HEADS, MLA_V))


def merge(y_ssd, y_mla, g_mla, gate_raw, w_ps, w_pm, w_o):
    y_mla = y_mla * jax.nn.silu(g_mla)
    g = jax.nn.sigmoid(gate_raw.astype(jnp.float32)).astype(y_ssd.dtype)
    return (g[..., :D_MODEL] * (y_ssd @ w_ps) + g[..., D_MODEL:] * (y_mla @ w_pm)) @ w_o


def prompt_layer(x, p, l):
    b, L, _ = x.shape
    z, xbc, dt_raw, cq, ckv_raw, kr_raw, g_mla, gate_raw = in_project(x, p['norm_w'][l], p['w_in'][l])
    ctx0 = jnp.zeros((b, SSD_CONV - 1, SSD_CONV_DIM), xbc.dtype)
    xbc_c, conv_state = causal_conv(xbc, ctx0, p['conv_w'][l], p['conv_b'][l])
    xs, dt, bm, cm = ssd_inputs(xbc_c, dt_raw, p['dt_bias'][l])
    A = -jnp.exp(p['a_log'][l].astype(jnp.float32))
    s0 = jnp.zeros((b, SSD_HEADS, SSD_HEAD_DIM, SSD_STATE), jnp.float32)
    y_m, s_m = ssd_chunked(xs[:, :N_META], dt[:, :N_META], A, bm[:, :N_META], cm[:, :N_META], s0, N_META)
    y_r, s_f = ssd_chunked(xs[:, N_META:], dt[:, N_META:], A, bm[:, N_META:], cm[:, N_META:], s_m, SSD_CHUNK)
    y_ssd = ssd_output(jnp.concatenate([y_m, y_r], 1), xs, z, p['d_skip'][l], p['ssd_norm_w'][l])
    pos = jnp.arange(L)
    q_nope, q_rope, c_kv, k_rope = mla_project(cq, ckv_raw, kr_raw, pos, p['q_norm_w'][l], p['w_uq'][l], p['kv_norm_w'][l])
    y_mla = mla_prompt_attn(q_nope, q_rope, c_kv, k_rope, p['w_uk'][l], p['w_uv'][l]).reshape(b, L, MLA_INNER).astype(x.dtype)
    out = merge(y_ssd, y_mla, g_mla, gate_raw, p['w_proj_ssd'][l], p['w_proj_mla'][l], p['w_out'][l])
    return x + out, (c_kv, k_rope, s_f, conv_state)


def sample_layer(x, conv_ctx, ssm_state, cache_ckv, cache_kr, page_table, p, l):
    b, T, _ = x.shape
    past_len = page_table.shape[1] * cache_ckv.shape[2]
    z, xbc, dt_raw, cq, ckv_raw, kr_raw, g_mla, gate_raw = in_project(x, p['norm_w'][l], p['w_in'][l])
    xbc_c, conv_state = causal_conv(xbc, conv_ctx.astype(xbc.dtype), p['conv_w'][l], p['conv_b'][l])
    xs, dt, bm, cm = ssd_inputs(xbc_c, dt_raw, p['dt_bias'][l])
    A = -jnp.exp(p['a_log'][l].astype(jnp.float32))
    y, s_f = ssd_recurrent(xs, dt, A, bm, cm, ssm_state.astype(jnp.float32))
    y_ssd = ssd_output(y, xs, z, p['d_skip'][l], p['ssd_norm_w'][l])
    pos = past_len + jnp.arange(T)
    q_nope, q_rope, c_kv, k_rope = mla_project(cq, ckv_raw, kr_raw, pos, p['q_norm_w'][l], p['w_uq'][l], p['kv_norm_w'][l])
    y_mla = mla_sample_attn(q_nope, q_rope, c_kv, k_rope, cache_ckv, cache_kr, page_table, l,
                            p['w_uk'][l], p['w_uv'][l]).reshape(b, T, MLA_INNER).astype(x.dtype)
    out = merge(y_ssd, y_mla, g_mla, gate_raw, p['w_proj_ssd'][l], p['w_proj_mla'][l], p['w_out'][l])
    return x + out, (c_kv, k_rope, s_f, conv_state)


def setup_inputs(seed: int = 0) -> dict:
    key = jax.random.key(seed)
    ks = list(jax.random.split(key, 32))
    f32 = jnp.float32

    def nrm(shape, scale):
        return jax.random.normal(ks.pop(), shape, f32) * scale

    def gain(shape):
        return 1.0 + nrm(shape, 0.01)

    n_pages = PAST_LEN // PAGE_SIZE
    n_phys = (DEC_BATCH * n_pages * 5) // 4
    x_prompt = nrm((BATCH, SEQ, D_MODEL), 1.0)
    x_sample = nrm((DEC_BATCH, DEC_SEQ, D_MODEL), 1.0)
    cache_ckv = nrm((DEPTH, n_phys, PAGE_SIZE, MLA_KV_LORA), 1.0)
    cache_kr = nrm((DEPTH, n_phys, PAGE_SIZE, MLA_ROPE), 1.0)
    state_ssm = nrm((DEPTH, DEC_BATCH, SSD_HEADS, SSD_HEAD_DIM, SSD_STATE), 0.1)
    state_conv = nrm((DEPTH, DEC_BATCH, SSD_CONV - 1, SSD_CONV_DIM), 1.0)
    page_table = jax.random.permutation(ks.pop(), n_phys)[:DEC_BATCH * n_pages].reshape(DEC_BATCH, n_pages).astype(jnp.int32)
    meta_tokens = nrm((N_META, D_MODEL), 1.0)
    norm_w = gain((DEPTH, D_MODEL))
    w_in = nrm((DEPTH, D_MODEL, IN_DIM), D_MODEL ** -0.5)
    conv_w = nrm((DEPTH, SSD_CONV, SSD_CONV_DIM), SSD_CONV ** -0.5)
    conv_b = nrm((DEPTH, SSD_CONV_DIM), 0.01)
    dt0 = jnp.exp(jax.random.uniform(ks.pop(), (DEPTH, SSD_HEADS), f32, math.log(1e-3), math.log(1e-1)))
    dt_bias = dt0 + jnp.log(-jnp.expm1(-dt0))
    a_log = jnp.log(jax.random.uniform(ks.pop(), (DEPTH, SSD_HEADS), f32, 1.0, 16.0))
    d_skip = gain((DEPTH, SSD_HEADS))
    ssd_norm_w = gain((DEPTH, SSD_INNER))
    q_norm_w = gain((DEPTH, MLA_Q_LORA))
    w_uq = nrm((DEPTH, MLA_Q_LORA, MLA_HEADS * (MLA_NOPE + MLA_ROPE)), MLA_Q_LORA ** -0.5)
    kv_norm_w = gain((DEPTH, MLA_KV_LORA))
    w_uk = nrm((DEPTH, MLA_KV_LORA, MLA_HEADS * MLA_NOPE), MLA_KV_LORA ** -0.5)
    w_uv = nrm((DEPTH, MLA_KV_LORA, MLA_HEADS * MLA_V), MLA_KV_LORA ** -0.5)
    w_proj_ssd = nrm((DEPTH, SSD_INNER, D_MODEL), SSD_INNER ** -0.5)
    w_proj_mla = nrm((DEPTH, MLA_INNER, D_MODEL), MLA_INNER ** -0.5)
    w_out = nrm((DEPTH, D_MODEL, D_MODEL), D_MODEL ** -0.5)
    final_norm_w = gain((D_MODEL,))
    return {'x_prompt': x_prompt, 'x_sample': x_sample, 'cache_ckv': cache_ckv, 'cache_kr': cache_kr,
            'state_ssm': state_ssm, 'state_conv': state_conv, 'page_table': page_table,
            'meta_tokens': meta_tokens, 'norm_w': norm_w, 'w_in': w_in, 'conv_w': conv_w, 'conv_b': conv_b,
            'dt_bias': dt_bias, 'a_log': a_log, 'd_skip': d_skip, 'ssd_norm_w': ssd_norm_w,
            'q_norm_w': q_norm_w, 'w_uq': w_uq, 'kv_norm_w': kv_norm_w, 'w_uk': w_uk, 'w_uv': w_uv,
            'w_proj_ssd': w_proj_ssd, 'w_proj_mla': w_proj_mla, 'w_out': w_out, 'final_norm_w': final_norm_w}


def reference(x_prompt, x_sample, cache_ckv, cache_kr, state_ssm, state_conv, page_table,
              meta_tokens, norm_w, w_in, conv_w, conv_b, dt_bias, a_log, d_skip, ssd_norm_w,
              q_norm_w, w_uq, kv_norm_w, w_uk, w_uv, w_proj_ssd, w_proj_mla, w_out, final_norm_w):
    p = {'norm_w': norm_w, 'w_in': w_in, 'conv_w': conv_w, 'conv_b': conv_b, 'dt_bias': dt_bias,
         'a_log': a_log, 'd_skip': d_skip, 'ssd_norm_w': ssd_norm_w, 'q_norm_w': q_norm_w, 'w_uq': w_uq,
         'kv_norm_w': kv_norm_w, 'w_uk': w_uk, 'w_uv': w_uv, 'w_proj_ssd': w_proj_ssd,
         'w_proj_mla': w_proj_mla, 'w_out': w_out}
    b = x_prompt.shape[0]
    meta = jnp.broadcast_to(meta_tokens[None].astype(x_prompt.dtype), (b, N_META, D_MODEL))
    hp = jnp.concatenate([meta, x_prompt], 1)
    hs = x_sample
    st_p, st_s = [], []
    for l in range(DEPTH):
        hp, sp = prompt_layer(hp, p, l)
        hs, ss = sample_layer(hs, state_conv[l], state_ssm[l], cache_ckv, cache_kr, page_table, p, l)
        st_p.append(sp)
        st_s.append(ss)
    y_prompt = rmsnorm(hp[:, N_META:], final_norm_w)
    y_sample = rmsnorm(hs, final_norm_w)
    dt_p, dt_s = x_prompt.dtype, x_sample.dtype
    ckv_prompt = jnp.stack([s[0] for s in st_p]).astype(dt_p)
    kr_prompt = jnp.stack([s[1] for s in st_p]).astype(dt_p)
    ssm_prompt = jnp.stack([s[2] for s in st_p]).astype(dt_p)
    conv_prompt = jnp.stack([s[3] for s in st_p]).astype(dt_p)
    ckv_sample = jnp.stack([s[0] for s in st_s]).astype(dt_s)
    kr_sample = jnp.stack([s[1] for s in st_s]).astype(dt_s)
    ssm_sample = jnp.stack([s[2] for s in st_s]).astype(dt_s)
    conv_sample = jnp.stack([s[3] for s in st_s]).astype(dt_s)
    return (y_prompt, y_sample, ckv_prompt, kr_prompt, ssm_prompt, conv_prompt, ckv_sample, kr_sample, ssm_sample, conv_sample)
```

```python
import functools
import math

import jax
import jax.numpy as jnp
from jax import lax
from jax.experimental import pallas as pl
from jax.experimental.pallas import tpu as pltpu

N_META = 16
NORM_EPS = 1e-6
SSD_HEADS = 16
SSD_HEAD_DIM = 64
SSD_INNER = SSD_HEADS * SSD_HEAD_DIM
SSD_GROUPS = 4
HEADS_PER_GROUP = SSD_HEADS // SSD_GROUPS
SSD_STATE = 128
SSD_CONV = 4
SSD_CHUNK = 128
SSD_BC = SSD_GROUPS * SSD_STATE
SSD_CONV_DIM = SSD_INNER + 2 * SSD_BC
MLA_HEADS = 8
MLA_NOPE = 64
MLA_ROPE = 32
MLA_V = 64
MLA_Q_LORA = 384
MLA_KV_LORA = 256
MLA_INNER = MLA_HEADS * MLA_V
MLA_SCALE = (MLA_NOPE + MLA_ROPE) ** -0.5
ROPE_BASE = 10000.0

LANES = 128
HEAD_PAD = LANES
ROPE_LANE0 = MLA_NOPE
ROPE_HALF = MLA_ROPE // 2
CTX_ROW0 = 8 - (SSD_CONV - 1)
VMEM_LIMIT = 56 * 1024 * 1024

BF16 = jnp.bfloat16
F32 = jnp.float32


def _cparams(sem):
    return pltpu.CompilerParams(dimension_semantics=sem, vmem_limit_bytes=VMEM_LIMIT)


def _full(shape):
    n = len(shape)
    return pl.BlockSpec(shape, lambda *_: (0,) * n)


def _rms(x, w):
    return x * lax.rsqrt(jnp.mean(x * x, -1, keepdims=True) + NORM_EPS) * w


def _silu(x):
    return x * (1.0 / (1.0 + jnp.exp(-x)))


def _sigmoid(x):
    return 1.0 / (1.0 + jnp.exp(-x))


def _softplus(x):
    return jnp.maximum(x, 0.0) + jnp.log(1.0 + jnp.exp(-jnp.abs(x)))


def _dot(a, b):
    return jnp.dot(a, b, preferred_element_type=F32)


def _dot_nt(a, b):
    return lax.dot_general(a, b, (((1,), (1,)), ((), ())), preferred_element_type=F32)


def _dot_tn(a, b):
    return lax.dot_general(a, b, (((0,), (0,)), ((), ())), preferred_element_type=F32)


def _split3(a):
    a1 = a.astype(BF16)
    r = a - a1.astype(F32)
    a2 = r.astype(BF16)
    a3 = (r - a2.astype(F32)).astype(BF16)
    return a1, a2, a3


def _dot_sel(a, sel):
    a1, a2, a3 = _split3(a)
    return _dot(a1, sel) + _dot(a2, sel) + _dot(a3, sel)


def _sel_dot(sel, a):
    a1, a2, a3 = _split3(a)
    return _dot(sel, a1) + _dot(sel, a2) + _dot(sel, a3)


def _head_expand():
    r = lax.broadcasted_iota(jnp.int32, (SSD_HEADS, SSD_INNER), 0)
    c = lax.broadcasted_iota(jnp.int32, (SSD_HEADS, SSD_INNER), 1)
    return jnp.where(c // SSD_HEAD_DIM == r, 1.0, 0.0).astype(BF16)


def _rope_lanes(x, c, s1, s2):
    n = x.shape[-1]
    return x * c + pltpu.roll(x, n - ROPE_HALF, 1) * s1 + pltpu.roll(x, ROPE_HALF, 1) * s2


def _seq_inproj_kernel(x_ref, normw_ref, wxbc_ref, convw_ref, convb_ref, ctx0_ref,
                       wdt_ref, dtb_ref, wdtT_ref, dtbT_ref,
                       wcq_ref, qnw_ref, wuq_ref, wckv_ref, kvnw_ref, wuk_ref, wuv_ref, wkr_ref,
                       rc_ref, rs1_ref, rs2_ref,
                       xbc_out, dt_out, dtT_out, q_out, k_out, v_out, ckv_out, kr_out, cs_out,
                       cbuf):
    t = pl.program_id(1)
    T = x_ref.shape[1]
    nctx = SSD_CONV - 1
    h = _rms(x_ref[0], normw_ref[...]).astype(BF16)

    @pl.when(t == 0)
    def _():
        cbuf[CTX_ROW0:8, :] = ctx0_ref[...]

    cbuf[8:8 + T, :] = _dot(h, wxbc_ref[...])
    acc = convb_ref[...]
    for k in range(SSD_CONV):
        acc = acc + cbuf[CTX_ROW0 + k:CTX_ROW0 + k + T, :] * convw_ref[k:k + 1, :]
    xbc_out[0] = _silu(acc)
    last = cbuf[8 + T - nctx:8 + T, :]
    cs_out[0] = last
    cbuf[CTX_ROW0:8, :] = last

    dt_out[0] = _softplus(_dot(h, wdt_ref[...]) + dtb_ref[...])
    dtT_out[0] = _softplus(_dot_nt(wdtT_ref[...], h) + dtbT_ref[...])

    rc, rs1, rs2 = rc_ref[...], rs1_ref[...], rs2_ref[...]
    cq = _rms(_dot(h, wcq_ref[...]), qnw_ref[...]).astype(BF16)
    q = _dot(cq, wuq_ref[...])
    q = _rope_lanes(q, jnp.tile(rc, (1, MLA_HEADS)), jnp.tile(rs1, (1, MLA_HEADS)),
                    jnp.tile(rs2, (1, MLA_HEADS)))
    q_out[0] = (q * MLA_SCALE).astype(BF16)

    kr = _rope_lanes(_dot(h, wkr_ref[...]), rc, rs1, rs2)
    kr_out[0] = kr[:, ROPE_LANE0:ROPE_LANE0 + MLA_ROPE]
    ckv = _rms(_dot(h, wckv_ref[...]), kvnw_ref[...])
    ckv_out[0] = ckv
    cb = ckv.astype(BF16)
    k_out[0] = (_dot(cb, wuk_ref[...]) + jnp.tile(kr, (1, MLA_HEADS))).astype(BF16)
    v_out[0] = _dot(cb, wuv_ref[...]).astype(BF16)


def _seq_inproj(x, ctx0, rope_tabs, lw, tile):
    B, L, D = x.shape
    T = min(tile, L)
    rc, rs1, rs2 = rope_tabs
    row = lambda w: pl.BlockSpec((1, T, w), lambda b, t: (b, t, 0))
    tab = pl.BlockSpec((T, LANES), lambda b, t: (t, 0))
    weights = [lw['norm_w'], lw['w_xbc'], lw['conv_w'], lw['conv_b'], ctx0,
               lw['w_dt'], lw['dt_bias'], lw['w_dtT'], lw['dt_biasT'],
               lw['w_cq'], lw['q_norm_w'], lw['w_uq_p'], lw['w_ckv'], lw['kv_norm_w'],
               lw['w_uk_p'], lw['w_uv'], lw['w_kr_p']]
    out_shape = (
        jax.ShapeDtypeStruct((B, L, SSD_CONV_DIM), F32),
        jax.ShapeDtypeStruct((B, L, SSD_HEADS), F32),
        jax.ShapeDtypeStruct((B, SSD_HEADS, L), F32),
        jax.ShapeDtypeStruct((B, L, MLA_HEADS * HEAD_PAD), BF16),
        jax.ShapeDtypeStruct((B, L, MLA_HEADS * HEAD_PAD), BF16),
        jax.ShapeDtypeStruct((B, L, MLA_INNER), BF16),
        jax.ShapeDtypeStruct((B, L, MLA_KV_LORA), F32),
        jax.ShapeDtypeStruct((B, L, MLA_ROPE), F32),
        jax.ShapeDtypeStruct((B, SSD_CONV - 1, SSD_CONV_DIM), F32),
    )
    out_specs = (row(SSD_CONV_DIM), row(SSD_HEADS),
                 pl.BlockSpec((1, SSD_HEADS, T), lambda b, t: (b, 0, t)),
                 row(MLA_HEADS * HEAD_PAD), row(MLA_HEADS * HEAD_PAD), row(MLA_INNER),
                 row(MLA_KV_LORA), row(MLA_ROPE),
                 pl.BlockSpec((1, SSD_CONV - 1, SSD_CONV_DIM), lambda b, t: (b, 0, 0)))
    return pl.pallas_call(
        _seq_inproj_kernel,
        grid=(B, L // T),
        in_specs=[row(D)] + [_full(w.shape) for w in weights] + [tab, tab, tab],
        out_specs=out_specs,
        out_shape=out_shape,
        scratch_shapes=[pltpu.VMEM((8 + T, SSD_CONV_DIM), F32)],
        compiler_params=_cparams(("parallel", "arbitrary")),
    )(x, *weights, rc, rs1, rs2)


def _ssd_seq_kernel(xbc_ref, dt_ref, dtT_ref, alog_ref, alogT_ref, dskip_ref, s0_ref,
                    y_out, sT_out, sT):
    c = pl.program_id(1)
    Q = xbc_ref.shape[1]
    GW = HEADS_PER_GROUP * SSD_HEAD_DIM

    @pl.when(c == 0)
    def _():
        sT[...] = s0_ref[...]

    dt = dt_ref[0]
    a = dt * (-jnp.exp(alog_ref[...]))
    aT = dtT_ref[0] * (-jnp.exp(alogT_ref[...]))
    ri = lax.broadcasted_iota(jnp.int32, (Q, Q), 0)
    ci = lax.broadcasted_iota(jnp.int32, (Q, Q), 1)
    causal = ci <= ri
    tril = jnp.where(causal, 1.0, 0.0).astype(BF16)
    triu = jnp.where(ri <= ci, 1.0, 0.0).astype(BF16)
    acs = _sel_dot(tril, a)
    acsT = _dot_sel(aT, triu)
    expand = _head_expand()
    dt_e = _dot_sel(dt, expand)
    acs_e = _dot_sel(acs, expand)
    end_e = acs_e[Q - 1:Q, :]
    xs = xbc_ref[0, :, 0:SSD_INNER]
    xdt = xs * dt_e
    xw = (xdt * jnp.exp(end_e - acs_e)).astype(BF16)
    xdt_b = xdt.astype(BF16)
    in_scale = jnp.exp(acs_e)
    chunk_decay = jnp.exp(end_e)
    dxs = xs * dskip_ref[...]

    for g in range(SSD_GROUPS):
        lo = g * GW
        Bg = xbc_ref[0, :, SSD_INNER + g * SSD_STATE:SSD_INNER + (g + 1) * SSD_STATE].astype(BF16)
        Cg = xbc_ref[0, :, SSD_INNER + SSD_BC + g * SSD_STATE:
                     SSD_INNER + SSD_BC + (g + 1) * SSD_STATE].astype(BF16)
        cb = _dot_nt(Cg, Bg)
        y_off = _dot(Cg, sT[:, lo:lo + GW].astype(BF16)) * in_scale[:, lo:lo + GW]
        ys = []
        for e in range(HEADS_PER_GROUP):
            hh = g * HEADS_PER_GROUP + e
            diff = acs[:, hh:hh + 1] - acsT[hh:hh + 1, :]
            m = (cb * jnp.exp(jnp.where(causal, diff, -jnp.inf))).astype(BF16)
            ys.append(_dot(m, xdt_b[:, hh * SSD_HEAD_DIM:(hh + 1) * SSD_HEAD_DIM]))
        y_out[0, :, lo:lo + GW] = jnp.concatenate(ys, axis=1) + y_off + dxs[:, lo:lo + GW]
        sT[:, lo:lo + GW] = sT[:, lo:lo + GW] * chunk_decay[:, lo:lo + GW] + _dot_tn(Bg, xw[:, lo:lo + GW])

    sT_out[0] = sT[...]


def _ssd_seq(xbc_c, dt, dtT, s0T, lw, chunk):
    B, L, _ = xbc_c.shape
    Q = min(chunk, L)
    return pl.pallas_call(
        _ssd_seq_kernel,
        grid=(B, L // Q),
        in_specs=[pl.BlockSpec((1, Q, SSD_CONV_DIM), lambda b, c: (b, c, 0)),
                  pl.BlockSpec((1, Q, SSD_HEADS), lambda b, c: (b, c, 0)),
                  pl.BlockSpec((1, SSD_HEADS, Q), lambda b, c: (b, 0, c)),
                  _full((1, SSD_HEADS)), _full((SSD_HEADS, 1)), _full((1, SSD_INNER)),
                  _full((SSD_STATE, SSD_INNER))],
        out_specs=(pl.BlockSpec((1, Q, SSD_INNER), lambda b, c: (b, c, 0)),
                   pl.BlockSpec((1, SSD_STATE, SSD_INNER), lambda b, c: (b, 0, 0))),
        out_shape=(jax.ShapeDtypeStruct((B, L, SSD_INNER), F32),
                   jax.ShapeDtypeStruct((B, SSD_STATE, SSD_INNER), F32)),
        scratch_shapes=[pltpu.VMEM((SSD_STATE, SSD_INNER), F32)],
        compiler_params=_cparams(("parallel", "arbitrary")),
    )(xbc_c, dt, dtT, lw['a_log'], lw['a_logT'], lw['d_skip_e'], s0T)


def _online(carry, s, v):
    m, l, acc = carry
    m_new = jnp.maximum(m, jnp.max(s, -1, keepdims=True))
    corr = jnp.exp(m - m_new)
    p = jnp.exp(s - m_new)
    return (m_new, l * corr + jnp.sum(p, -1, keepdims=True),
            acc * corr + _dot(p.astype(BF16), v))


def _attn_seq_kernel(*refs, n_pre):
    if n_pre:
        q_ref, k_ref, v_ref, kpre_ref, vpre_ref, o_ref = refs
    else:
        q_ref, k_ref, v_ref, o_ref = refs
    i = pl.program_id(1)
    TQ = q_ref.shape[1]
    ri = lax.broadcasted_iota(jnp.int32, (TQ, TQ), 0)
    ci = lax.broadcasted_iota(jnp.int32, (TQ, TQ), 1)
    lane = lax.broadcasted_iota(jnp.int32, (TQ, 2 * MLA_V), 1)
    row0 = pl.multiple_of(i * TQ, TQ)

    for hp in range(MLA_HEADS // 2):
        outs = []
        for hh in (2 * hp, 2 * hp + 1):
            ksl = slice(hh * HEAD_PAD, (hh + 1) * HEAD_PAD)
            vsl = slice(hp * 2 * MLA_V, (hp + 1) * 2 * MLA_V)
            qh = q_ref[0, :, ksl]
            carry = (jnp.full((TQ, 1), -jnp.inf, F32), jnp.zeros((TQ, 1), F32),
                     jnp.zeros((TQ, 2 * MLA_V), F32))
            if n_pre:
                carry = _online(carry, _dot_nt(qh, kpre_ref[:, ksl]), vpre_ref[:, vsl])

            def body(j, carry, qh=qh, ksl=ksl, vsl=vsl):
                r = pl.multiple_of(j * TQ, TQ)
                s = _dot_nt(qh, k_ref[0, pl.ds(r, TQ), ksl])
                return _online(carry, s, v_ref[0, pl.ds(r, TQ), vsl])

            carry = lax.fori_loop(0, i, body, carry)
            s = _dot_nt(qh, k_ref[0, pl.ds(row0, TQ), ksl])
            m, l, acc = _online(carry, jnp.where(ci <= ri, s, -jnp.inf), v_ref[0, pl.ds(row0, TQ), vsl])
            outs.append(acc * (1.0 / l))
        o_ref[0, :, hp * 2 * MLA_V:(hp + 1) * 2 * MLA_V] = jnp.where(lane < MLA_V, outs[0], outs[1])


def _attn_seq(q, k, v, kpre, vpre, tile):
    B, L, _ = q.shape
    TQ = min(tile, L)
    n_pre = 0 if kpre is None else kpre.shape[0]
    seq = lambda w: pl.BlockSpec((1, L, w), lambda b, i: (b, 0, 0))
    in_specs = [pl.BlockSpec((1, TQ, MLA_HEADS * HEAD_PAD), lambda b, i: (b, i, 0)),
                seq(MLA_HEADS * HEAD_PAD), seq(MLA_INNER)]
    args = [q, k, v]
    if n_pre:
        in_specs += [_full(kpre.shape), _full(vpre.shape)]
        args += [kpre, vpre]
    return pl.pallas_call(
        functools.partial(_attn_seq_kernel, n_pre=n_pre),
        grid=(B, L // TQ),
        in_specs=in_specs,
        out_specs=pl.BlockSpec((1, TQ, MLA_INNER), lambda b, i: (b, i, 0)),
        out_shape=jax.ShapeDtypeStruct((B, L, MLA_INNER), F32),
        compiler_params=_cparams(("parallel", "arbitrary")),
    )(*args)


def _merge_kernel(x_ref, yssd_ref, ymla_ref, normw_ref, wz_ref, wg_ref, wgate_ref, ssdnw_ref,
                  wps_ref, wpm_ref, wo_ref, fnw_ref, o_ref, *, final):
    x = x_ref[...]
    h = _rms(x, normw_ref[...]).astype(BF16)
    z = _dot(h, wz_ref[...])
    y_ssd = _rms(yssd_ref[...] * _silu(z), ssdnw_ref[...]).astype(BF16)
    y_mla = (ymla_ref[...] * _silu(_dot(h, wg_ref[...]))).astype(BF16)
    gate = _sigmoid(_dot(h, wgate_ref[...]))
    D = x.shape[-1]
    u = gate[:, :D] * _dot(y_ssd, wps_ref[...]) + gate[:, D:] * _dot(y_mla, wpm_ref[...])
    out = x + _dot(u.astype(BF16), wo_ref[...])
    if final:
        out = _rms(out, fnw_ref[...])
    o_ref[...] = out


def _merge_rows(x, y_ssd, y_mla, lw, fnw, final, tile):
    R, D = x.shape
    T = min(tile, R)
    row = lambda w: pl.BlockSpec((T, w), lambda r: (r, 0))
    weights = [lw['norm_w'], lw['w_z'], lw['w_g'], lw['w_gate'], lw['ssd_norm_w'],
               lw['w_proj_ssd'], lw['w_proj_mla'], lw['w_out'], fnw]
    return pl.pallas_call(
        functools.partial(_merge_kernel, final=final),
        grid=(R // T,),
        in_specs=[row(D), row(SSD_INNER), row(MLA_INNER)] + [_full(w.shape) for w in weights],
        out_specs=row(D),
        out_shape=jax.ShapeDtypeStruct((R, D), F32),
        compiler_params=_cparams(("parallel",)),
    )(x, y_ssd, y_mla, *weights)


def _sample_inproj_kernel(x_ref, normw_ref, wxbc_ref, convw_ref, convb_ref, ctx_ref,
                          wdt_ref, dtb_ref, wcq_ref, qnw_ref, wuqn_ref, wuqr_ref, wukT_ref,
                          wckv_ref, kvnw_ref, wkr_ref, rc_ref, rs1_ref, rs2_ref,
                          xbc_out, dt_out, cs_out, qlat_out, qr_out, ckv_out, kr_out):
    C = SSD_CONV_DIM
    h = _rms(x_ref[...], normw_ref[...]).astype(BF16)
    raw = _dot(h, wxbc_ref[...])
    acc = convb_ref[...] + raw * convw_ref[SSD_CONV - 1:SSD_CONV, :]
    for k in range(SSD_CONV - 1):
        acc = acc + ctx_ref[:, k * C:(k + 1) * C] * convw_ref[k:k + 1, :]
    xbc_out[...] = _silu(acc)
    cs_out[:, 0:(SSD_CONV - 2) * C] = ctx_ref[:, C:(SSD_CONV - 1) * C]
    cs_out[:, (SSD_CONV - 2) * C:] = raw
    dt_out[...] = _softplus(_dot(h, wdt_ref[...]) + dtb_ref[...])

    rc, rs1, rs2 = rc_ref[...], rs1_ref[...], rs2_ref[...]
    cq = _rms(_dot(h, wcq_ref[...]), qnw_ref[...]).astype(BF16)
    qn = (_dot(cq, wuqn_ref[...]) * MLA_SCALE).astype(BF16)
    qr = _rope_lanes(_dot(cq, wuqr_ref[...]), jnp.tile(rc, (1, MLA_HEADS)),
                     jnp.tile(rs1, (1, MLA_HEADS)), jnp.tile(rs2, (1, MLA_HEADS))) * MLA_SCALE
    for hh in range(MLA_HEADS):
        qlat_out[:, hh, :] = _dot(qn[:, hh * MLA_NOPE:(hh + 1) * MLA_NOPE], wukT_ref[hh])
        qr_out[:, hh, :] = qr[:, hh * LANES + ROPE_LANE0:hh * LANES + ROPE_LANE0 + MLA_ROPE]
    kr = _rope_lanes(_dot(h, wkr_ref[...]), rc, rs1, rs2)
    kr_out[...] = kr[:, ROPE_LANE0:ROPE_LANE0 + MLA_ROPE]
    ckv_out[...] = _rms(_dot(h, wckv_ref[...]), kvnw_ref[...])


def _sample_inproj(x, ctx, rope_tabs, lw):
    R, D = x.shape
    rc, rs1, rs2 = rope_tabs
    args = [x, lw['norm_w'], lw['w_xbc'], lw['conv_w'], lw['conv_b'], ctx, lw['w_dt'], lw['dt_bias'],
            lw['w_cq'], lw['q_norm_w'], lw['w_uq_nope'], lw['w_uq_rope_p'], lw['w_ukT'],
            lw['w_ckv'], lw['kv_norm_w'], lw['w_kr_p'], rc, rs1, rs2]
    out_shape = (jax.ShapeDtypeStruct((R, SSD_CONV_DIM), F32),
                 jax.ShapeDtypeStruct((R, SSD_HEADS), F32),
                 jax.ShapeDtypeStruct((R, (SSD_CONV - 1) * SSD_CONV_DIM), F32),
                 jax.ShapeDtypeStruct((R, MLA_HEADS, MLA_KV_LORA), F32),
                 jax.ShapeDtypeStruct((R, MLA_HEADS, MLA_ROPE), F32),
                 jax.ShapeDtypeStruct((R, MLA_KV_LORA), F32),
                 jax.ShapeDtypeStruct((R, MLA_ROPE), F32))
    return pl.pallas_call(
        _sample_inproj_kernel,
        grid=(1,),
        in_specs=[_full(a.shape) for a in args],
        out_specs=tuple(_full(s.shape) for s in out_shape),
        out_shape=out_shape,
        compiler_params=_cparams(("arbitrary",)),
    )(*args)


def _ssd_step_kernel(xbc_ref, dt_ref, alog_ref, dskip_ref, s_ref, y_out, s_out):
    NB = xbc_ref.shape[0]
    GW = HEADS_PER_GROUP * SSD_HEAD_DIM
    expand = _head_expand()
    dt_e = _dot_sel(dt_ref[...], expand)
    a_e = _dot_sel(-jnp.exp(alog_ref[...]), expand)
    xs = xbc_ref[:, 0:SSD_INNER]
    xdt = xs * dt_e
    dec = jnp.exp(dt_e * a_e)
    dxs = xs * dskip_ref[...]
    KR = 16
    row = lax.broadcasted_iota(jnp.int32, (KR, SSD_INNER), 0)
    col = lax.broadcasted_iota(jnp.int32, (KR, SSD_INNER), 1)
    gmask = jnp.where((row < 2 * SSD_GROUPS) & (col // GW == row % SSD_GROUPS), 1.0, 0.0)
    rown = lax.broadcasted_iota(jnp.int32, (KR, SSD_STATE), 0)
    dec_rows = jnp.where((rown >= 2 * SSD_GROUPS) & (rown < 2 * SSD_GROUPS + 3), 1.0, 0.0)
    for b in range(NB):
        xb = xdt[b:b + 1, :]
        x1 = xb.astype(BF16).astype(F32)
        d1, d2, d3 = [d.astype(F32) for d in _split3(dec[b:b + 1, :])]
        lhs = jnp.where(row < SSD_GROUPS, x1, xb - x1) * gmask
        lhs = lhs + jnp.where(row == 2 * SSD_GROUPS, d1,
                              jnp.where(row == 2 * SSD_GROUPS + 1, d2,
                                        jnp.where(row == 2 * SSD_GROUPS + 2, d3, 0.0)))
        bmat = jnp.zeros((KR, SSD_STATE), F32)
        cmat = jnp.zeros((KR, SSD_STATE), F32)
        for g in range(SSD_GROUPS):
            bg = xbc_ref[b:b + 1, SSD_INNER + g * SSD_STATE:SSD_INNER + (g + 1) * SSD_STATE]
            cg = xbc_ref[b:b + 1, SSD_INNER + SSD_BC + g * SSD_STATE:SSD_INNER + SSD_BC + (g + 1) * SSD_STATE]
            bmat = bmat + jnp.where((rown < 2 * SSD_GROUPS) & (rown % SSD_GROUPS == g), bg, 0.0)
            cmat = cmat + jnp.where(rown == g, cg, 0.0)
        rhs = jnp.concatenate([bmat, dec_rows], axis=1).astype(BF16)
        r = _dot_tn(lhs.astype(BF16), rhs)
        s_new = s_ref[b] * r[:, SSD_STATE:] + r[:, :SSD_STATE]
        s_out[b] = s_new
        yt = _dot_nt(cmat.astype(BF16), s_new.astype(BF16))
        y_out[b:b + 1, :] = jnp.sum(yt * gmask, axis=0, keepdims=True) + dxs[b:b + 1, :]


def _ssd_step(xbc_c, dt, state, lw, nb):
    R = xbc_c.shape[0]
    return pl.pallas_call(
        _ssd_step_kernel,
        grid=(R // nb,),
        in_specs=[pl.BlockSpec((nb, SSD_CONV_DIM), lambda i: (i, 0)),
                  pl.BlockSpec((nb, SSD_HEADS), lambda i: (i, 0)),
                  _full((1, SSD_HEADS)), _full((1, SSD_INNER)),
                  pl.BlockSpec((nb, SSD_INNER, SSD_STATE), lambda i: (i, 0, 0))],
        out_specs=(pl.BlockSpec((nb, SSD_INNER), lambda i: (i, 0)),
                   pl.BlockSpec((nb, SSD_INNER, SSD_STATE), lambda i: (i, 0, 0))),
        out_shape=(jax.ShapeDtypeStruct((R, SSD_INNER), F32),
                   jax.ShapeDtypeStruct((R, SSD_INNER, SSD_STATE), F32)),
        compiler_params=_cparams(("parallel",)),
    )(xbc_c, dt, lw['a_log'], lw['d_skip_e'], state)


def _decode_kernel(pt_ref, qlat_ref, qr_ref, ckvn_ref, krn_ref, cckv_hbm, ckr_hbm, o_ref,
                   ckv_buf, kr_buf, sems, m_s, l_s, acc_s, *, layer, G):
    b = pl.program_id(0)
    c = pl.program_id(1)
    NB = pl.num_programs(0)
    NC = pl.num_programs(1)
    step = b * NC + c
    slot = lax.rem(step, 2)

    def copies(bb, cc, sl):
        out = []
        for g in range(G):
            page = pt_ref[bb, cc * G + g]
            out.append(pltpu.make_async_copy(cckv_hbm.at[layer, page], ckv_buf.at[sl, g], sems.at[sl, 0]))
            out.append(pltpu.make_async_copy(ckr_hbm.at[layer, page], kr_buf.at[sl, g], sems.at[sl, 1]))
        return out

    @pl.when(step == 0)
    def _():
        for cp in copies(b, c, slot):
            cp.start()

    for cp in copies(b, c, slot):
        cp.wait()

    @pl.when(step + 1 < NB * NC)
    def _():
        wrap = c + 1 == NC
        for cp in copies(jnp.where(wrap, b + 1, b), jnp.where(wrap, 0, c + 1), 1 - slot):
            cp.start()

    @pl.when(c == 0)
    def _():
        m_s[...] = jnp.full(m_s.shape, -jnp.inf, F32)
        l_s[...] = jnp.zeros(l_s.shape, F32)
        acc_s[...] = jnp.zeros(acc_s.shape, F32)

    ql = qlat_ref[0].astype(BF16)
    qr = qr_ref[0].astype(BF16)
    kc = ckv_buf[slot].reshape(G * ckv_buf.shape[2], MLA_KV_LORA).astype(BF16)
    kr = kr_buf[slot].reshape(G * kr_buf.shape[2], MLA_ROPE).astype(BF16)
    s = _dot_nt(ql, kc) + _dot_nt(qr, kr)
    m, l, acc = _online((m_s[...], l_s[...], acc_s[...]), s, kc)
    m_s[...], l_s[...], acc_s[...] = m, l, acc

    @pl.when(c == NC - 1)
    def _():
        kn = ckvn_ref[0]
        s1 = (jnp.sum(qlat_ref[0] * kn, -1, keepdims=True)
              + jnp.sum(qr_ref[0] * krn_ref[0], -1, keepdims=True))
        m2 = jnp.maximum(m, s1)
        corr = jnp.exp(m - m2)
        p1 = jnp.exp(s1 - m2)
        o_ref[0] = (acc * corr + p1 * kn) * (1.0 / (l * corr + p1))


def _decode_attn(page_table, qlat, qr, ckv_new, kr_new, cache_ckv, cache_kr, layer, G):
    R, n_pages = page_table.shape
    page = cache_ckv.shape[2]
    G = min(G, n_pages)
    grid_spec = pltpu.PrefetchScalarGridSpec(
        num_scalar_prefetch=1,
        grid=(R, n_pages // G),
        in_specs=[pl.BlockSpec((1, MLA_HEADS, MLA_KV_LORA), lambda b, c, pt: (b, 0, 0)),
                  pl.BlockSpec((1, MLA_HEADS, MLA_ROPE), lambda b, c, pt: (b, 0, 0)),
                  pl.BlockSpec((1, 1, MLA_KV_LORA), lambda b, c, pt: (b, 0, 0)),
                  pl.BlockSpec((1, 1, MLA_ROPE), lambda b, c, pt: (b, 0, 0)),
                  pl.BlockSpec(memory_space=pl.ANY),
                  pl.BlockSpec(memory_space=pl.ANY)],
        out_specs=pl.BlockSpec((1, MLA_HEADS, MLA_KV_LORA), lambda b, c, pt: (b, 0, 0)),
        scratch_shapes=[pltpu.VMEM((2, G, page, MLA_KV_LORA), F32),
                        pltpu.VMEM((2, G, page, MLA_ROPE), F32),
                        pltpu.SemaphoreType.DMA((2, 2)),
                        pltpu.VMEM((MLA_HEADS, 1), F32),
                        pltpu.VMEM((MLA_HEADS, 1), F32),
                        pltpu.VMEM((MLA_HEADS, MLA_KV_LORA), F32)])
    return pl.pallas_call(
        functools.partial(_decode_kernel, layer=layer, G=G),
        grid_spec=grid_spec,
        out_shape=jax.ShapeDtypeStruct((R, MLA_HEADS, MLA_KV_LORA), F32),
        compiler_params=_cparams(("arbitrary", "arbitrary")),
    )(page_table, qlat, qr, ckv_new.reshape(R, 1, MLA_KV_LORA), kr_new.reshape(R, 1, MLA_ROPE),
      cache_ckv, cache_kr)


def _uv_proj_kernel(o_ref, wuv_ref, y_ref):
    for hh in range(MLA_HEADS):
        y_ref[:, hh * MLA_V:(hh + 1) * MLA_V] = _dot(o_ref[:, hh, :].astype(BF16), wuv_ref[hh])


def _uv_proj(o_lat, w_uv_h):
    R = o_lat.shape[0]
    return pl.pallas_call(
        _uv_proj_kernel,
        grid=(1,),
        in_specs=[_full(o_lat.shape), _full(w_uv_h.shape)],
        out_specs=_full((R, MLA_INNER)),
        out_shape=jax.ShapeDtypeStruct((R, MLA_INNER), F32),
        compiler_params=_cparams(("arbitrary",)),
    )(o_lat, w_uv_h)


def _rope_tables(pos, lane0):
    inv = 1.0 / (ROPE_BASE ** (jnp.arange(ROPE_HALF, dtype=F32) / ROPE_HALF))
    ang = pos.astype(F32)[:, None] * inv[None, :]
    cos, sin = jnp.cos(ang), jnp.sin(ang)
    n = pos.shape[0]
    z = lambda w: jnp.zeros((n, w), F32)
    pre = jnp.ones((n, lane0), F32)
    tail = LANES - lane0 - MLA_ROPE
    rc = jnp.concatenate([pre, cos, cos, z(tail)], 1)
    rs1 = jnp.concatenate([z(lane0), -sin, z(ROPE_HALF), z(tail)], 1)
    rs2 = jnp.concatenate([z(lane0), z(ROPE_HALF), sin, z(tail)], 1)
    return rc, rs1, rs2


def _layer_weights(p, l):
    D = p['w_in'].shape[1]
    w_in = p['w_in'][l]
    offs = [0]
    for n in (SSD_INNER, SSD_CONV_DIM, SSD_HEADS, MLA_Q_LORA, MLA_KV_LORA, MLA_ROPE, MLA_INNER, 2 * D):
        offs.append(offs[-1] + n)
    w_z, w_xbc, w_dt, w_cq, w_ckv, w_kr, w_g, w_gate = [w_in[:, offs[i]:offs[i + 1]] for i in range(8)]
    bf = lambda a: a.astype(BF16)
    pad_head = lambda a: jnp.pad(a, ((0, 0), (0, 0), (0, HEAD_PAD - a.shape[-1])))
    w_uq = p['w_uq'][l].reshape(MLA_Q_LORA, MLA_HEADS, MLA_NOPE + MLA_ROPE)
    w_uk = p['w_uk'][l].reshape(MLA_KV_LORA, MLA_HEADS, MLA_NOPE)
    w_uv = p['w_uv'][l]
    w_kr_p = jnp.pad(w_kr, ((0, 0), (ROPE_LANE0, LANES - ROPE_LANE0 - MLA_ROPE)))
    w_uq_rope_p = jnp.pad(w_uq[:, :, MLA_NOPE:], ((0, 0), (0, 0), (ROPE_LANE0, LANES - ROPE_LANE0 - MLA_ROPE)))
    return {
        'norm_w': p['norm_w'][l][None], 'w_xbc': bf(w_xbc), 'conv_w': p['conv_w'][l],
        'conv_b': p['conv_b'][l][None], 'w_dt': bf(w_dt), 'dt_bias': p['dt_bias'][l][None],
        'w_dtT': bf(w_dt.T), 'dt_biasT': p['dt_bias'][l][:, None],
        'w_cq': bf(w_cq), 'q_norm_w': p['q_norm_w'][l][None],
        'w_uq_p': bf(pad_head(w_uq).reshape(MLA_Q_LORA, MLA_HEADS * HEAD_PAD)),
        'w_uq_nope': bf(w_uq[:, :, :MLA_NOPE].reshape(MLA_Q_LORA, MLA_HEADS * MLA_NOPE)),
        'w_uq_rope_p': bf(w_uq_rope_p.reshape(MLA_Q_LORA, MLA_HEADS * LANES)),
        'w_ckv': bf(w_ckv), 'kv_norm_w': p['kv_norm_w'][l][None],
        'w_uk_p': bf(pad_head(w_uk).reshape(MLA_KV_LORA, MLA_HEADS * HEAD_PAD)),
        'w_ukT': bf(jnp.transpose(w_uk, (1, 2, 0))),
        'w_uv': bf(w_uv),
        'w_uv_h': bf(jnp.transpose(w_uv.reshape(MLA_KV_LORA, MLA_HEADS, MLA_V), (1, 0, 2))),
        'w_kr_p': bf(w_kr_p),
        'a_log': p['a_log'][l][None], 'a_logT': p['a_log'][l][:, None],
        'd_skip_e': jnp.repeat(p['d_skip'][l], SSD_HEAD_DIM)[None],
        'w_z': bf(w_z), 'w_g': bf(w_g), 'w_gate': bf(w_gate),
        'ssd_norm_w': p['ssd_norm_w'][l][None],
        'w_proj_ssd': bf(p['w_proj_ssd'][l]), 'w_proj_mla': bf(p['w_proj_mla'][l]), 'w_out': bf(p['w_out'][l]),
    }


SEQ_TILE = 256
ATTN_TILE = 256
MERGE_TILE = 256
STEP_ROWS = 8
PAGES_PER_STEP = 16


def _seq_layer(x, ctx0, s0T, kpre, vpre, rope_tabs, lw, fnw, final, chunk, need_out=True):
    B, L, D = x.shape
    xbc_c, dt, dtT, q, k, v, ckv, kr, conv_state = _seq_inproj(x, ctx0, rope_tabs, lw, SEQ_TILE)
    y_ssd, sT = _ssd_seq(xbc_c, dt, dtT, s0T, lw, chunk)
    if not need_out:
        return None, (ckv, kr, sT, conv_state, k, v)
    y_mla = _attn_seq(q, k, v, kpre, vpre, ATTN_TILE)
    out = _merge_rows(x.reshape(B * L, D), y_ssd.reshape(B * L, SSD_INNER), y_mla.reshape(B * L, MLA_INNER),
                      lw, fnw, final, MERGE_TILE).reshape(B, L, D)
    return out, (ckv, kr, sT, conv_state, k, v)


def kernel(x_prompt, x_sample, cache_ckv, cache_kr, state_ssm, state_conv, page_table, meta_tokens,
           norm_w, w_in, conv_w, conv_b, dt_bias, a_log, d_skip, ssd_norm_w, q_norm_w, w_uq, kv_norm_w,
           w_uk, w_uv, w_proj_ssd, w_proj_mla, w_out, final_norm_w):
    p = {'norm_w': norm_w, 'w_in': w_in, 'conv_w': conv_w, 'conv_b': conv_b, 'dt_bias': dt_bias,
         'a_log': a_log, 'd_skip': d_skip, 'ssd_norm_w': ssd_norm_w, 'q_norm_w': q_norm_w, 'w_uq': w_uq,
         'kv_norm_w': kv_norm_w, 'w_uk': w_uk, 'w_uv': w_uv, 'w_proj_ssd': w_proj_ssd,
         'w_proj_mla': w_proj_mla, 'w_out': w_out}
    depth = w_in.shape[0]
    B, L, D = x_prompt.shape
    R = x_sample.shape[0]
    n_meta = meta_tokens.shape[0]
    past_len = page_table.shape[1] * cache_ckv.shape[2]
    fnw = final_norm_w[None]

    tabs_meta = _rope_tables(jnp.arange(n_meta), ROPE_LANE0)
    tabs_prompt = _rope_tables(n_meta + jnp.arange(L), ROPE_LANE0)
    tabs_sample = _rope_tables(past_len + jnp.arange(1), ROPE_LANE0)

    hm = meta_tokens[None].astype(x_prompt.dtype)
    hp = x_prompt
    hs = x_sample.reshape(R, D)
    outs = {k: [] for k in ('ckv_p', 'kr_p', 'ssm_p', 'conv_p', 'ckv_s', 'kr_s', 'ssm_s', 'conv_s')}
    for l in range(depth):
        lw = _layer_weights(p, l)
        final = l == depth - 1
        zero_ctx = jnp.zeros((SSD_CONV - 1, SSD_CONV_DIM), F32)
        zero_state = jnp.zeros((SSD_STATE, SSD_INNER), F32)
        hm_next, (ckv_m, kr_m, sT_m, cs_m, k_m, v_m) = _seq_layer(
            hm, zero_ctx, zero_state, None, None, tabs_meta, lw, fnw, False, n_meta, need_out=not final)
        hp, (ckv_pp, kr_pp, sT_p, cs_p, _, _) = _seq_layer(
            hp, cs_m[0], sT_m[0], k_m[0], v_m[0], tabs_prompt, lw, fnw, final, SSD_CHUNK)
        hm = hm_next
        outs['ckv_p'].append(jnp.concatenate([jnp.broadcast_to(ckv_m, (B,) + ckv_m.shape[1:]), ckv_pp], 1))
        outs['kr_p'].append(jnp.concatenate([jnp.broadcast_to(kr_m, (B,) + kr_m.shape[1:]), kr_pp], 1))
        outs['ssm_p'].append(jnp.transpose(sT_p.reshape(B, SSD_STATE, SSD_HEADS, SSD_HEAD_DIM), (0, 2, 3, 1)))
        outs['conv_p'].append(cs_p)

        ctx = state_conv[l].reshape(R, (SSD_CONV - 1) * SSD_CONV_DIM)
        xbc_c, dt, cs_s, qlat, qr, ckv_s, kr_s = _sample_inproj(hs, ctx, tabs_sample, lw)
        y_ssd, s_new = _ssd_step(xbc_c, dt, state_ssm[l].reshape(R, SSD_INNER, SSD_STATE), lw, STEP_ROWS)
        o_lat = _decode_attn(page_table, qlat, qr, ckv_s, kr_s, cache_ckv, cache_kr, l, PAGES_PER_STEP)
        y_mla = _uv_proj(o_lat, lw['w_uv_h'])
        hs = _merge_rows(hs, y_ssd, y_mla, lw, fnw, final, MERGE_TILE)
        outs['ckv_s'].append(ckv_s.reshape(R, 1, MLA_KV_LORA))
        outs['kr_s'].append(kr_s.reshape(R, 1, MLA_ROPE))
        outs['ssm_s'].append(s_new.reshape(R, SSD_HEADS, SSD_HEAD_DIM, SSD_STATE))
        outs['conv_s'].append(cs_s.reshape(R, SSD_CONV - 1, SSD_CONV_DIM))

    st = lambda k: jnp.stack(outs[k])
    return (hp, hs.reshape(R, 1, D), st('ckv_p'), st('kr_p'), st('ssm_p'), st('conv_p'),
            st('ckv_s'), st('kr_s'), st('ssm_s'), st('conv_s'))
```

```python
import functools
import math

import jax
import jax.numpy as jnp
from jax import lax
from jax.experimental import pallas as pl
from jax.experimental.pallas import tpu as pltpu

N_META = 16
NORM_EPS = 1e-6
SSD_HEADS = 16
SSD_HEAD_DIM = 64
SSD_INNER = SSD_HEADS * SSD_HEAD_DIM
SSD_GROUPS = 4
HEADS_PER_GROUP = SSD_HEADS // SSD_GROUPS
SSD_STATE = 128
SSD_CONV = 4
SSD_CHUNK = 128
SSD_BC = SSD_GROUPS * SSD_STATE
SSD_CONV_DIM = SSD_INNER + 2 * SSD_BC
MLA_HEADS = 8
MLA_NOPE = 64
MLA_ROPE = 32
MLA_V = 64
MLA_Q_LORA = 384
MLA_KV_LORA = 256
MLA_INNER = MLA_HEADS * MLA_V
MLA_SCALE = (MLA_NOPE + MLA_ROPE) ** -0.5
ROPE_BASE = 10000.0
LOG2E = math.log2(math.e)

LANES = 128
HEAD_PAD = LANES
ROPE_LANE0 = MLA_NOPE
ROPE_HALF = MLA_ROPE // 2
CTX_ROW0 = 8 - (SSD_CONV - 1)
VMEM_LIMIT = 56 * 1024 * 1024

BF16 = jnp.bfloat16
F32 = jnp.float32


def _cparams(sem):
    return pltpu.CompilerParams(dimension_semantics=sem, vmem_limit_bytes=VMEM_LIMIT)


def _full(shape):
    n = len(shape)
    return pl.BlockSpec(shape, lambda *_: (0,) * n)


def _rms(x, w):
    return x * lax.rsqrt(jnp.mean(x * x, -1, keepdims=True) + NORM_EPS) * w


def _silu(x):
    return x * (1.0 / (1.0 + jnp.exp(-x)))


def _sigmoid(x):
    return 1.0 / (1.0 + jnp.exp(-x))


def _softplus(x):
    return jnp.maximum(x, 0.0) + jnp.log(1.0 + jnp.exp(-jnp.abs(x)))


def _dot(a, b):
    return jnp.dot(a, b, preferred_element_type=F32)


def _dot_nt(a, b):
    return lax.dot_general(a, b, (((1,), (1,)), ((), ())), preferred_element_type=F32)


def _dot_tn(a, b):
    return lax.dot_general(a, b, (((0,), (0,)), ((), ())), preferred_element_type=F32)


def _heads(x):
    return jnp.stack([x[:, hh * HEAD_PAD:(hh + 1) * HEAD_PAD] for hh in range(MLA_HEADS)])


def _bdot_nt(a, b):
    return lax.dot_general(a, b, (((2,), (2,)), ((0,), (0,))), preferred_element_type=F32)


def _bdot(a, b):
    return lax.dot_general(a, b, (((2,), (1,)), ((0,), (0,))), preferred_element_type=F32)


def _split3(a):
    a1 = a.astype(BF16)
    r = a - a1.astype(F32)
    a2 = r.astype(BF16)
    a3 = (r - a2.astype(F32)).astype(BF16)
    return a1, a2, a3


def _dot_sel(a, sel):
    a1, a2, a3 = _split3(a)
    return _dot(a1, sel) + _dot(a2, sel) + _dot(a3, sel)


def _sel_dot(sel, a):
    a1, a2, a3 = _split3(a)
    return _dot(sel, a1) + _dot(sel, a2) + _dot(sel, a3)


def _head_expand():
    r = lax.broadcasted_iota(jnp.int32, (SSD_HEADS, SSD_INNER), 0)
    c = lax.broadcasted_iota(jnp.int32, (SSD_HEADS, SSD_INNER), 1)
    return jnp.where(c // SSD_HEAD_DIM == r, 1.0, 0.0).astype(BF16)


def _rope_lanes(x, c, s1, s2):
    n = x.shape[-1]
    return x * c + pltpu.roll(x, n - ROPE_HALF, 1) * s1 + pltpu.roll(x, ROPE_HALF, 1) * s2


def _seq_inproj_kernel(x_ref, normw_ref, wxbc_ref, convw_ref, convb_ref, ctx0_ref,
                       wdt_ref, dtb_ref, wdtT_ref, dtbT_ref,
                       wcq_ref, qnw_ref, wuq_ref, wckv_ref, kvnw_ref, wuk_ref, wuv_ref, vone_ref, wkr_ref,
                       rc_ref, rs1_ref, rs2_ref,
                       xbc_out, dt_out, dtT_out, q_out, k_out, v_out, ckv_out, kr_out, cs_out,
                       cbuf):
    t = pl.program_id(1)
    T = x_ref.shape[1]
    nctx = SSD_CONV - 1
    h = _rms(x_ref[0], normw_ref[...]).astype(BF16)

    @pl.when(t == 0)
    def _():
        cbuf[CTX_ROW0:8, :] = ctx0_ref[...]

    cbuf[8:8 + T, :] = _dot(h, wxbc_ref[...])
    acc = convb_ref[...]
    for k in range(SSD_CONV):
        acc = acc + cbuf[CTX_ROW0 + k:CTX_ROW0 + k + T, :] * convw_ref[k:k + 1, :]
    xbc_out[0] = _silu(acc)
    last = cbuf[8 + T - nctx:8 + T, :]
    cs_out[0] = last
    cbuf[CTX_ROW0:8, :] = last

    dt_out[0] = _softplus(_dot(h, wdt_ref[...]) + dtb_ref[...])
    dtT_out[0] = _softplus(_dot_nt(wdtT_ref[...], h) + dtbT_ref[...])

    rc, rs1, rs2 = rc_ref[...], rs1_ref[...], rs2_ref[...]
    cq = _rms(_dot(h, wcq_ref[...]), qnw_ref[...]).astype(BF16)
    q = _dot(cq, wuq_ref[...])
    q = _rope_lanes(q, jnp.tile(rc, (1, MLA_HEADS)), jnp.tile(rs1, (1, MLA_HEADS)),
                    jnp.tile(rs2, (1, MLA_HEADS)))
    q_out[0] = (q * (MLA_SCALE * LOG2E)).astype(BF16)

    kr = _rope_lanes(_dot(h, wkr_ref[...]), rc, rs1, rs2)
    kr_out[0] = kr[:, ROPE_LANE0:ROPE_LANE0 + MLA_ROPE]
    ckv = _rms(_dot(h, wckv_ref[...]), kvnw_ref[...])
    ckv_out[0] = ckv
    cb = ckv.astype(BF16)
    k_out[0] = (_dot(cb, wuk_ref[...]) + jnp.tile(kr, (1, MLA_HEADS))).astype(BF16)
    v_out[0] = (_dot(cb, wuv_ref[...]) + vone_ref[...]).astype(BF16)


def _seq_inproj(x, ctx0, rope_tabs, lw, tile):
    B, L, D = x.shape
    T = min(tile, L)
    rc, rs1, rs2 = rope_tabs
    row = lambda w: pl.BlockSpec((1, T, w), lambda b, t: (b, t, 0))
    tab = pl.BlockSpec((T, LANES), lambda b, t: (t, 0))
    weights = [lw['norm_w'], lw['w_xbc'], lw['conv_w'], lw['conv_b'], ctx0,
               lw['w_dt'], lw['dt_bias'], lw['w_dtT'], lw['dt_biasT'],
               lw['w_cq'], lw['q_norm_w'], lw['w_uq_p'], lw['w_ckv'], lw['kv_norm_w'],
               lw['w_uk_p'], lw['w_uv_p'], lw['v_ones'], lw['w_kr_p']]
    out_shape = (
        jax.ShapeDtypeStruct((B, L, SSD_CONV_DIM), F32),
        jax.ShapeDtypeStruct((B, L, SSD_HEADS), F32),
        jax.ShapeDtypeStruct((B, SSD_HEADS, L), F32),
        jax.ShapeDtypeStruct((B, L, MLA_HEADS * HEAD_PAD), BF16),
        jax.ShapeDtypeStruct((B, L, MLA_HEADS * HEAD_PAD), BF16),
        jax.ShapeDtypeStruct((B, L, MLA_HEADS * HEAD_PAD), BF16),
        jax.ShapeDtypeStruct((B, L, MLA_KV_LORA), F32),
        jax.ShapeDtypeStruct((B, L, MLA_ROPE), F32),
        jax.ShapeDtypeStruct((B, SSD_CONV - 1, SSD_CONV_DIM), F32),
    )
    out_specs = (row(SSD_CONV_DIM), row(SSD_HEADS),
                 pl.BlockSpec((1, SSD_HEADS, T), lambda b, t: (b, 0, t)),
                 row(MLA_HEADS * HEAD_PAD), row(MLA_HEADS * HEAD_PAD), row(MLA_HEADS * HEAD_PAD),
                 row(MLA_KV_LORA), row(MLA_ROPE),
                 pl.BlockSpec((1, SSD_CONV - 1, SSD_CONV_DIM), lambda b, t: (b, 0, 0)))
    return pl.pallas_call(
        _seq_inproj_kernel,
        grid=(B, L // T),
        in_specs=[row(D)] + [_full(w.shape) for w in weights] + [tab, tab, tab],
        out_specs=out_specs,
        out_shape=out_shape,
        scratch_shapes=[pltpu.VMEM((8 + T, SSD_CONV_DIM), F32)],
        compiler_params=_cparams(("parallel", "arbitrary")),
    )(x, *weights, rc, rs1, rs2)


def _ssd_seq_kernel(xbc_ref, dt_ref, dtT_ref, alog_ref, alogT_ref, dskip_ref, s0_ref,
                    y_out, sT_out, sT):
    c = pl.program_id(1)
    Q = xbc_ref.shape[1]
    GW = HEADS_PER_GROUP * SSD_HEAD_DIM

    @pl.when(c == 0)
    def _():
        sT[...] = s0_ref[...]

    dt = dt_ref[0]
    a = dt * (-jnp.exp(alog_ref[...]))
    aT = dtT_ref[0] * (-jnp.exp(alogT_ref[...]))
    ri = lax.broadcasted_iota(jnp.int32, (Q, Q), 0)
    ci = lax.broadcasted_iota(jnp.int32, (Q, Q), 1)
    causal = ci <= ri
    tril = jnp.where(causal, 1.0, 0.0).astype(BF16)
    triu = jnp.where(ri <= ci, 1.0, 0.0).astype(BF16)
    acs = _sel_dot(tril, a)
    acsT = _dot_sel(aT, triu)
    expand = _head_expand()
    dt_e = _dot_sel(dt, expand)
    acs_e = _dot_sel(acs, expand)
    end_e = acs_e[Q - 1:Q, :]
    xs = xbc_ref[0, :, 0:SSD_INNER]
    xdt = xs * dt_e
    xw = (xdt * jnp.exp(end_e - acs_e)).astype(BF16)
    xdt_b = xdt.astype(BF16)
    in_scale = jnp.exp(acs_e)
    chunk_decay = jnp.exp(end_e)
    dxs = xs * dskip_ref[...]

    for g in range(SSD_GROUPS):
        lo = g * GW
        Bg = xbc_ref[0, :, SSD_INNER + g * SSD_STATE:SSD_INNER + (g + 1) * SSD_STATE].astype(BF16)
        Cg = xbc_ref[0, :, SSD_INNER + SSD_BC + g * SSD_STATE:
                     SSD_INNER + SSD_BC + (g + 1) * SSD_STATE].astype(BF16)
        cb = _dot_nt(Cg, Bg)
        y_off = _dot(Cg, sT[:, lo:lo + GW].astype(BF16)) * in_scale[:, lo:lo + GW]
        ys = []
        for e in range(HEADS_PER_GROUP):
            hh = g * HEADS_PER_GROUP + e
            diff = acs[:, hh:hh + 1] - acsT[hh:hh + 1, :]
            m = (cb * jnp.exp(jnp.where(causal, diff, -jnp.inf))).astype(BF16)
            ys.append(_dot(m, xdt_b[:, hh * SSD_HEAD_DIM:(hh + 1) * SSD_HEAD_DIM]))
        y_out[0, :, lo:lo + GW] = jnp.concatenate(ys, axis=1) + y_off + dxs[:, lo:lo + GW]
        sT[:, lo:lo + GW] = sT[:, lo:lo + GW] * chunk_decay[:, lo:lo + GW] + _dot_tn(Bg, xw[:, lo:lo + GW])

    sT_out[0] = sT[...]


def _ssd_seq(xbc_c, dt, dtT, s0T, lw, chunk):
    B, L, _ = xbc_c.shape
    Q = min(chunk, L)
    return pl.pallas_call(
        _ssd_seq_kernel,
        grid=(B, L // Q),
        in_specs=[pl.BlockSpec((1, Q, SSD_CONV_DIM), lambda b, c: (b, c, 0)),
                  pl.BlockSpec((1, Q, SSD_HEADS), lambda b, c: (b, c, 0)),
                  pl.BlockSpec((1, SSD_HEADS, Q), lambda b, c: (b, 0, c)),
                  _full((1, SSD_HEADS)), _full((SSD_HEADS, 1)), _full((1, SSD_INNER)),
                  _full((SSD_STATE, SSD_INNER))],
        out_specs=(pl.BlockSpec((1, Q, SSD_INNER), lambda b, c: (b, c, 0)),
                   pl.BlockSpec((1, SSD_STATE, SSD_INNER), lambda b, c: (b, 0, 0))),
        out_shape=(jax.ShapeDtypeStruct((B, L, SSD_INNER), F32),
                   jax.ShapeDtypeStruct((B, SSD_STATE, SSD_INNER), F32)),
        scratch_shapes=[pltpu.VMEM((SSD_STATE, SSD_INNER), F32)],
        compiler_params=_cparams(("parallel", "arbitrary")),
    )(xbc_c, dt, dtT, lw['a_log'], lw['a_logT'], lw['d_skip_e'], s0T)


def _attn_seq_kernel(*refs, n_pre):
    if n_pre:
        q_ref, k_ref, v_ref, kpre_ref, vpre_ref, o_ref, m_s, acc_s = refs
    else:
        q_ref, k_ref, v_ref, o_ref, m_s, acc_s = refs
    i = pl.program_id(1)
    TQ = q_ref.shape[1]
    causal = (lax.broadcasted_iota(jnp.int32, (TQ, TQ), 1)
              <= lax.broadcasted_iota(jnp.int32, (TQ, TQ), 0))[None]
    mshape = (MLA_HEADS, TQ, HEAD_PAD)

    def lanes_of(m, n):
        return m[..., :n] if n <= HEAD_PAD else jnp.concatenate([m] * (n // HEAD_PAD), -1)

    if n_pre:
        s = _bdot_nt(_heads(q_ref[0]), _heads(kpre_ref[...]))
        m = jnp.broadcast_to(jnp.max(s, -1, keepdims=True), mshape)
        m_s[...] = m
        acc_s[...] = _bdot(jnp.exp2(s - lanes_of(m, n_pre)).astype(BF16), _heads(vpre_ref[...]))
    else:
        m_s[...] = jnp.full(mshape, -jnp.inf, F32)
        acc_s[...] = jnp.zeros(mshape, F32)

    def step(r, masked):
        s = _bdot_nt(_heads(q_ref[0]), _heads(k_ref[0, pl.ds(r, TQ), :]))
        if masked:
            s = jnp.where(causal, s, -jnp.inf)
        m_old = m_s[...]
        m_new = jnp.maximum(m_old, jnp.max(s, -1, keepdims=True))
        p = jnp.exp2(s - lanes_of(m_new, TQ)).astype(BF16)
        acc_s[...] = acc_s[...] * jnp.exp2(m_old - m_new) + _bdot(p, _heads(v_ref[0, pl.ds(r, TQ), :]))
        m_s[...] = m_new

    def body(j, carry):
        step(pl.multiple_of(j * TQ, TQ), False)
        return carry

    lax.fori_loop(0, i, body, 0)
    step(pl.multiple_of(i * TQ, TQ), True)

    lane = lax.broadcasted_iota(jnp.int32, (TQ, HEAD_PAD), 1)
    for hp in range(MLA_HEADS // 2):
        a0, a1 = acc_s[2 * hp], acc_s[2 * hp + 1]
        o0 = a0 * (1.0 / a0[:, MLA_V:MLA_V + 1])
        o1 = a1 * (1.0 / a1[:, 0:1])
        o_ref[0, :, hp * HEAD_PAD:(hp + 1) * HEAD_PAD] = jnp.where(lane < MLA_V, o0, o1)


def _attn_seq(q, k, v, kpre, vpre, tile):
    B, L, W = q.shape
    TQ = min(tile, L)
    n_pre = 0 if kpre is None else kpre.shape[0]
    seq = pl.BlockSpec((1, L, W), lambda b, i: (b, 0, 0))
    in_specs = [pl.BlockSpec((1, TQ, W), lambda b, i: (b, i, 0)), seq, seq]
    args = [q, k, v]
    if n_pre:
        in_specs += [_full(kpre.shape), _full(vpre.shape)]
        args += [kpre, vpre]
    return pl.pallas_call(
        functools.partial(_attn_seq_kernel, n_pre=n_pre),
        grid=(B, L // TQ),
        in_specs=in_specs,
        out_specs=pl.BlockSpec((1, TQ, MLA_INNER), lambda b, i: (b, i, 0)),
        out_shape=jax.ShapeDtypeStruct((B, L, MLA_INNER), F32),
        scratch_shapes=[pltpu.VMEM((MLA_HEADS, TQ, HEAD_PAD), F32),
                        pltpu.VMEM((MLA_HEADS, TQ, HEAD_PAD), F32)],
        compiler_params=_cparams(("parallel", "arbitrary")),
    )(*args)


def _merge_kernel(x_ref, yssd_ref, ymla_ref, normw_ref, wz_ref, wg_ref, wgate_ref, ssdnw_ref,
                  wps_ref, wpm_ref, wo_ref, fnw_ref, o_ref, *, final):
    x = x_ref[...]
    h = _rms(x, normw_ref[...]).astype(BF16)
    z = _dot(h, wz_ref[...])
    y_ssd = _rms(yssd_ref[...] * _silu(z), ssdnw_ref[...]).astype(BF16)
    y_mla = (ymla_ref[...] * _silu(_dot(h, wg_ref[...]))).astype(BF16)
    gate = _sigmoid(_dot(h, wgate_ref[...]))
    D = x.shape[-1]
    u = gate[:, :D] * _dot(y_ssd, wps_ref[...]) + gate[:, D:] * _dot(y_mla, wpm_ref[...])
    out = x + _dot(u.astype(BF16), wo_ref[...])
    if final:
        out = _rms(out, fnw_ref[...])
    o_ref[...] = out


def _merge_rows(x, y_ssd, y_mla, lw, fnw, final, tile):
    R, D = x.shape
    T = min(tile, R)
    row = lambda w: pl.BlockSpec((T, w), lambda r: (r, 0))
    weights = [lw['norm_w'], lw['w_z'], lw['w_g'], lw['w_gate'], lw['ssd_norm_w'],
               lw['w_proj_ssd'], lw['w_proj_mla'], lw['w_out'], fnw]
    return pl.pallas_call(
        functools.partial(_merge_kernel, final=final),
        grid=(R // T,),
        in_specs=[row(D), row(SSD_INNER), row(MLA_INNER)] + [_full(w.shape) for w in weights],
        out_specs=row(D),
        out_shape=jax.ShapeDtypeStruct((R, D), F32),
        compiler_params=_cparams(("parallel",)),
    )(x, y_ssd, y_mla, *weights)


def _sample_inproj_kernel(x_ref, normw_ref, wxbc_ref, convw_ref, convb_ref, ctx_ref,
                          wdt_ref, dtb_ref, wcq_ref, qnw_ref, wuqn_ref, wuqr_ref, wukT_ref,
                          wckv_ref, kvnw_ref, wkr_ref, rc_ref, rs1_ref, rs2_ref,
                          xbc_out, dt_out, cs_out, qlat_out, qr_out, ckv_out, kr_out):
    C = SSD_CONV_DIM
    h = _rms(x_ref[...], normw_ref[...]).astype(BF16)
    raw = _dot(h, wxbc_ref[...])
    acc = convb_ref[...] + raw * convw_ref[SSD_CONV - 1:SSD_CONV, :]
    for k in range(SSD_CONV - 1):
        acc = acc + ctx_ref[k] * convw_ref[k:k + 1, :]
    xbc_out[...] = _silu(acc)
    for k in range(SSD_CONV - 2):
        cs_out[k] = ctx_ref[k + 1]
    cs_out[SSD_CONV - 2] = raw
    dt_out[...] = _softplus(_dot(h, wdt_ref[...]) + dtb_ref[...])

    rc, rs1, rs2 = rc_ref[...], rs1_ref[...], rs2_ref[...]
    cq = _rms(_dot(h, wcq_ref[...]), qnw_ref[...]).astype(BF16)
    qscale = MLA_SCALE * LOG2E
    qn = (_dot(cq, wuqn_ref[...]) * qscale).astype(BF16)
    qr = _rope_lanes(_dot(cq, wuqr_ref[...]), jnp.tile(rc, (1, MLA_HEADS)),
                     jnp.tile(rs1, (1, MLA_HEADS)), jnp.tile(rs2, (1, MLA_HEADS))) * qscale
    for hh in range(MLA_HEADS):
        qlat_out[:, hh, :] = _dot(qn[:, hh * MLA_NOPE:(hh + 1) * MLA_NOPE], wukT_ref[hh])
        qr_out[:, hh, :] = qr[:, hh * LANES + ROPE_LANE0:hh * LANES + ROPE_LANE0 + MLA_ROPE]
    kr = _rope_lanes(_dot(h, wkr_ref[...]), rc, rs1, rs2)
    kr_out[...] = kr[:, ROPE_LANE0:ROPE_LANE0 + MLA_ROPE]
    ckv_out[...] = _rms(_dot(h, wckv_ref[...]), kvnw_ref[...])


def _sample_inproj(x, ctx, rope_tabs, lw):
    R, D = x.shape
    rc, rs1, rs2 = rope_tabs
    args = [x, lw['norm_w'], lw['w_xbc'], lw['conv_w'], lw['conv_b'], ctx, lw['w_dt'], lw['dt_bias'],
            lw['w_cq'], lw['q_norm_w'], lw['w_uq_nope'], lw['w_uq_rope_p'], lw['w_ukT'],
            lw['w_ckv'], lw['kv_norm_w'], lw['w_kr_p'], rc, rs1, rs2]
    out_shape = (jax.ShapeDtypeStruct((R, SSD_CONV_DIM), F32),
                 jax.ShapeDtypeStruct((R, SSD_HEADS), F32),
                 jax.ShapeDtypeStruct((SSD_CONV - 1, R, SSD_CONV_DIM), F32),
                 jax.ShapeDtypeStruct((R, MLA_HEADS, MLA_KV_LORA), F32),
                 jax.ShapeDtypeStruct((R, MLA_HEADS, MLA_ROPE), F32),
                 jax.ShapeDtypeStruct((R, MLA_KV_LORA), F32),
                 jax.ShapeDtypeStruct((R, MLA_ROPE), F32))
    return pl.pallas_call(
        _sample_inproj_kernel,
        grid=(1,),
        in_specs=[_full(a.shape) for a in args],
        out_specs=tuple(_full(s.shape) for s in out_shape),
        out_shape=out_shape,
        compiler_params=_cparams(("arbitrary",)),
    )(*args)


def _ssd_step_kernel(xbc_ref, dt_ref, alog_ref, dskip_ref, s_ref, y_out, s_out):
    NB = xbc_ref.shape[0]
    GW = HEADS_PER_GROUP * SSD_HEAD_DIM
    expand = _head_expand()
    dt_e = _dot_sel(dt_ref[...], expand)
    a_e = _dot_sel(-jnp.exp(alog_ref[...]), expand)
    xs = xbc_ref[:, 0:SSD_INNER]
    xdt = xs * dt_e
    dec = jnp.exp(dt_e * a_e)
    dxs = xs * dskip_ref[...]
    KR = 16
    row = lax.broadcasted_iota(jnp.int32, (KR, SSD_INNER), 0)
    col = lax.broadcasted_iota(jnp.int32, (KR, SSD_INNER), 1)
    gmask = jnp.where((row < 2 * SSD_GROUPS) & (col // GW == row % SSD_GROUPS), 1.0, 0.0)
    rown = lax.broadcasted_iota(jnp.int32, (KR, SSD_STATE), 0)
    dec_rows = jnp.where((rown >= 2 * SSD_GROUPS) & (rown < 2 * SSD_GROUPS + 3), 1.0, 0.0)
    for b in range(NB):
        xb = xdt[b:b + 1, :]
        x1 = xb.astype(BF16).astype(F32)
        d1, d2, d3 = [d.astype(F32) for d in _split3(dec[b:b + 1, :])]
        lhs = jnp.where(row < SSD_GROUPS, x1, xb - x1) * gmask
        lhs = lhs + jnp.where(row == 2 * SSD_GROUPS, d1,
                              jnp.where(row == 2 * SSD_GROUPS + 1, d2,
                                        jnp.where(row == 2 * SSD_GROUPS + 2, d3, 0.0)))
        bmat = jnp.zeros((KR, SSD_STATE), F32)
        cmat = jnp.zeros((KR, SSD_STATE), F32)
        for g in range(SSD_GROUPS):
            bg = xbc_ref[b:b + 1, SSD_INNER + g * SSD_STATE:SSD_INNER + (g + 1) * SSD_STATE]
            cg = xbc_ref[b:b + 1, SSD_INNER + SSD_BC + g * SSD_STATE:SSD_INNER + SSD_BC + (g + 1) * SSD_STATE]
            bmat = bmat + jnp.where((rown < 2 * SSD_GROUPS) & (rown % SSD_GROUPS == g), bg, 0.0)
            cmat = cmat + jnp.where(rown == g, cg, 0.0)
        rhs = jnp.concatenate([bmat, dec_rows], axis=1).astype(BF16)
        r = _dot_tn(lhs.astype(BF16), rhs)
        s_new = s_ref[b] * r[:, SSD_STATE:] + r[:, :SSD_STATE]
        s_out[b] = s_new
        yt = _dot_nt(cmat.astype(BF16), s_new.astype(BF16))
        y_out[b:b + 1, :] = jnp.sum(yt * gmask, axis=0, keepdims=True) + dxs[b:b + 1, :]


def _ssd_step(xbc_c, dt, state, lw, nb):
    R = xbc_c.shape[0]
    return pl.pallas_call(
        _ssd_step_kernel,
        grid=(R // nb,),
        in_specs=[pl.BlockSpec((nb, SSD_CONV_DIM), lambda i: (i, 0)),
                  pl.BlockSpec((nb, SSD_HEADS), lambda i: (i, 0)),
                  _full((1, SSD_HEADS)), _full((1, SSD_INNER)),
                  pl.BlockSpec((nb, SSD_INNER, SSD_STATE), lambda i: (i, 0, 0))],
        out_specs=(pl.BlockSpec((nb, SSD_INNER), lambda i: (i, 0)),
                   pl.BlockSpec((nb, SSD_INNER, SSD_STATE), lambda i: (i, 0, 0))),
        out_shape=(jax.ShapeDtypeStruct((R, SSD_INNER), F32),
                   jax.ShapeDtypeStruct((R, SSD_INNER, SSD_STATE), F32)),
        compiler_params=_cparams(("parallel",)),
    )(xbc_c, dt, lw['a_log'], lw['d_skip_e'], state)


def _decode_kernel(pt_ref, qlat_ref, qr_ref, ckvn_ref, krn_ref, cckv_hbm, ckr_hbm, o_ref,
                   ckv_buf, kr_buf, sems, m_s, l_s, acc_s, *, layer, G):
    b = pl.program_id(0)
    c = pl.program_id(1)
    NB = pl.num_programs(0)
    NC = pl.num_programs(1)
    step = b * NC + c
    slot = lax.rem(step, 2)

    P = cckv_hbm.shape[2]

    def start_page(bb, cc, sl, g):
        page = pt_ref[bb, cc * G + g]
        r = pl.multiple_of(g * P, P)
        pltpu.make_async_copy(cckv_hbm.at[layer, page], ckv_buf.at[sl, pl.ds(r, P)], sems.at[sl, 0]).start()
        pltpu.make_async_copy(ckr_hbm.at[layer, page], kr_buf.at[sl, :, pl.ds(r, P)], sems.at[sl, 1]).start()

    @pl.when(step == 0)
    def _():
        def prime(g, carry):
            start_page(b, c, slot, g)
            return carry
        lax.fori_loop(0, G, prime, 0)

    pltpu.make_async_copy(ckv_buf.at[slot], ckv_buf.at[slot], sems.at[slot, 0]).wait()
    pltpu.make_async_copy(kr_buf.at[slot], kr_buf.at[slot], sems.at[slot, 1]).wait()

    @pl.when(step + 1 < NB * NC)
    def _():
        wrap = c + 1 == NC
        nb, nc = jnp.where(wrap, b + 1, b), jnp.where(wrap, 0, c + 1)
        for g in range(G):
            start_page(nb, nc, 1 - slot, g)

    @pl.when(c == 0)
    def _():
        m_s[...] = jnp.full(m_s.shape, -jnp.inf, F32)
        l_s[...] = jnp.zeros(l_s.shape, F32)
        acc_s[...] = jnp.zeros(acc_s.shape, F32)

    ql = qlat_ref[0].astype(BF16)
    qr = qr_ref[0].astype(BF16)
    kc = ckv_buf[slot].astype(BF16)
    s = _dot_nt(ql, kc) + _dot(qr, kr_buf[slot].astype(BF16))
    m, l, acc = m_s[...], l_s[...], acc_s[...]
    n_sub = 2 if G % 2 == 0 else 1
    W = G * P // n_sub
    for u in range(n_sub):
        su = s[:, u * W:(u + 1) * W]
        m_new = jnp.maximum(m, jnp.max(su, -1, keepdims=True))
        corr = jnp.exp2(m - m_new)
        p = jnp.exp2(su - m_new)
        l = l * corr + jnp.sum(p, -1, keepdims=True)
        acc = acc * corr + _dot(p.astype(BF16), kc[u * W:(u + 1) * W])
        m = m_new
    m_s[...], l_s[...], acc_s[...] = m, l, acc

    @pl.when(c == NC - 1)
    def _():
        kn = ckvn_ref[0]
        s1 = (jnp.sum(qlat_ref[0] * kn, -1, keepdims=True)
              + jnp.sum(qr_ref[0] * krn_ref[0], -1, keepdims=True))
        m2 = jnp.maximum(m, s1)
        corr2 = jnp.exp2(m - m2)
        p1 = jnp.exp2(s1 - m2)
        o_ref[0] = (acc * corr2 + p1 * kn) * (1.0 / (l * corr2 + p1))


def _decode_attn(page_table, qlat, qr, ckv_new, kr_new, cache_ckv, cache_kr, layer, G):
    R, n_pages = page_table.shape
    page = cache_ckv.shape[2]
    G = min(G, n_pages)
    grid_spec = pltpu.PrefetchScalarGridSpec(
        num_scalar_prefetch=1,
        grid=(R, n_pages // G),
        in_specs=[pl.BlockSpec((1, MLA_HEADS, MLA_KV_LORA), lambda b, c, pt: (b, 0, 0)),
                  pl.BlockSpec((1, MLA_HEADS, MLA_ROPE), lambda b, c, pt: (b, 0, 0)),
                  pl.BlockSpec((1, 1, MLA_KV_LORA), lambda b, c, pt: (b, 0, 0)),
                  pl.BlockSpec((1, 1, MLA_ROPE), lambda b, c, pt: (b, 0, 0)),
                  pl.BlockSpec(memory_space=pl.ANY),
                  pl.BlockSpec(memory_space=pl.ANY)],
        out_specs=pl.BlockSpec((1, MLA_HEADS, MLA_KV_LORA), lambda b, c, pt: (b, 0, 0)),
        scratch_shapes=[pltpu.VMEM((2, G * page, MLA_KV_LORA), F32),
                        pltpu.VMEM((2, MLA_ROPE, G * page), F32),
                        pltpu.SemaphoreType.DMA((2, 2)),
                        pltpu.VMEM((MLA_HEADS, 1), F32),
                        pltpu.VMEM((MLA_HEADS, 1), F32),
                        pltpu.VMEM((MLA_HEADS, MLA_KV_LORA), F32)])
    return pl.pallas_call(
        functools.partial(_decode_kernel, layer=layer, G=G),
        grid_spec=grid_spec,
        out_shape=jax.ShapeDtypeStruct((R, MLA_HEADS, MLA_KV_LORA), F32),
        compiler_params=_cparams(("arbitrary", "arbitrary")),
    )(page_table, qlat, qr, ckv_new.reshape(R, 1, MLA_KV_LORA), kr_new.reshape(R, 1, MLA_ROPE),
      cache_ckv, jnp.swapaxes(cache_kr, 2, 3))


def _uv_proj_kernel(o_ref, wuv_ref, y_ref):
    for hh in range(MLA_HEADS):
        y_ref[:, hh * MLA_V:(hh + 1) * MLA_V] = _dot(o_ref[:, hh, :].astype(BF16), wuv_ref[hh])


def _uv_proj(o_lat, w_uv_h):
    R = o_lat.shape[0]
    return pl.pallas_call(
        _uv_proj_kernel,
        grid=(1,),
        in_specs=[_full(o_lat.shape), _full(w_uv_h.shape)],
        out_specs=_full((R, MLA_INNER)),
        out_shape=jax.ShapeDtypeStruct((R, MLA_INNER), F32),
        compiler_params=_cparams(("arbitrary",)),
    )(o_lat, w_uv_h)


def _rope_tables(pos, lane0):
    inv = 1.0 / (ROPE_BASE ** (jnp.arange(ROPE_HALF, dtype=F32) / ROPE_HALF))
    ang = pos.astype(F32)[:, None] * inv[None, :]
    cos, sin = jnp.cos(ang), jnp.sin(ang)
    n = pos.shape[0]
    z = lambda w: jnp.zeros((n, w), F32)
    pre = jnp.ones((n, lane0), F32)
    tail = LANES - lane0 - MLA_ROPE
    rc = jnp.concatenate([pre, cos, cos, z(tail)], 1)
    rs1 = jnp.concatenate([z(lane0), -sin, z(ROPE_HALF), z(tail)], 1)
    rs2 = jnp.concatenate([z(lane0), z(ROPE_HALF), sin, z(tail)], 1)
    return rc, rs1, rs2


def _layer_weights(p, l):
    D = p['w_in'].shape[1]
    w_in = p['w_in'][l]
    offs = [0]
    for n in (SSD_INNER, SSD_CONV_DIM, SSD_HEADS, MLA_Q_LORA, MLA_KV_LORA, MLA_ROPE, MLA_INNER, 2 * D):
        offs.append(offs[-1] + n)
    w_z, w_xbc, w_dt, w_cq, w_ckv, w_kr, w_g, w_gate = [w_in[:, offs[i]:offs[i + 1]] for i in range(8)]
    bf = lambda a: a.astype(BF16)
    pad_head = lambda a: jnp.pad(a, ((0, 0), (0, 0), (0, HEAD_PAD - a.shape[-1])))
    w_uq = p['w_uq'][l].reshape(MLA_Q_LORA, MLA_HEADS, MLA_NOPE + MLA_ROPE)
    w_uk = p['w_uk'][l].reshape(MLA_KV_LORA, MLA_HEADS, MLA_NOPE)
    w_uv = p['w_uv'][l]
    w_uv_pairs = w_uv.reshape(MLA_KV_LORA, MLA_HEADS // 2, 2, MLA_V)
    zv = jnp.zeros((MLA_KV_LORA, MLA_HEADS // 2, MLA_V), F32)
    w_uv_p = jnp.stack([jnp.concatenate([w_uv_pairs[:, :, 0], zv], -1),
                        jnp.concatenate([zv, w_uv_pairs[:, :, 1]], -1)], 2)
    one_at = lambda lane: (jnp.arange(HEAD_PAD) == lane).astype(F32)
    v_ones = jnp.tile(jnp.concatenate([one_at(MLA_V), one_at(0)]), MLA_HEADS // 2)[None]
    w_kr_p = jnp.pad(w_kr, ((0, 0), (ROPE_LANE0, LANES - ROPE_LANE0 - MLA_ROPE)))
    w_uq_rope_p = jnp.pad(w_uq[:, :, MLA_NOPE:], ((0, 0), (0, 0), (ROPE_LANE0, LANES - ROPE_LANE0 - MLA_ROPE)))
    return {
        'norm_w': p['norm_w'][l][None], 'w_xbc': bf(w_xbc), 'conv_w': p['conv_w'][l],
        'conv_b': p['conv_b'][l][None], 'w_dt': bf(w_dt), 'dt_bias': p['dt_bias'][l][None],
        'w_dtT': bf(w_dt.T), 'dt_biasT': p['dt_bias'][l][:, None],
        'w_cq': bf(w_cq), 'q_norm_w': p['q_norm_w'][l][None],
        'w_uq_p': bf(pad_head(w_uq).reshape(MLA_Q_LORA, MLA_HEADS * HEAD_PAD)),
        'w_uq_nope': bf(w_uq[:, :, :MLA_NOPE].reshape(MLA_Q_LORA, MLA_HEADS * MLA_NOPE)),
        'w_uq_rope_p': bf(w_uq_rope_p.reshape(MLA_Q_LORA, MLA_HEADS * LANES)),
        'w_ckv': bf(w_ckv), 'kv_norm_w': p['kv_norm_w'][l][None],
        'w_uk_p': bf(pad_head(w_uk).reshape(MLA_KV_LORA, MLA_HEADS * HEAD_PAD)),
        'w_ukT': bf(jnp.transpose(w_uk, (1, 2, 0))),
        'w_uv_p': bf(w_uv_p.reshape(MLA_KV_LORA, MLA_HEADS * HEAD_PAD)), 'v_ones': v_ones,
        'w_uv_h': bf(jnp.transpose(w_uv.reshape(MLA_KV_LORA, MLA_HEADS, MLA_V), (1, 0, 2))),
        'w_kr_p': bf(w_kr_p),
        'a_log': p['a_log'][l][None], 'a_logT': p['a_log'][l][:, None],
        'd_skip_e': jnp.repeat(p['d_skip'][l], SSD_HEAD_DIM)[None],
        'w_z': bf(w_z), 'w_g': bf(w_g), 'w_gate': bf(w_gate),
        'ssd_norm_w': p['ssd_norm_w'][l][None],
        'w_proj_ssd': bf(p['w_proj_ssd'][l]), 'w_proj_mla': bf(p['w_proj_mla'][l]), 'w_out': bf(p['w_out'][l]),
    }


SEQ_TILE = 256
ATTN_TILE = 256
MERGE_TILE = 256
STEP_ROWS = 8
PAGES_PER_STEP = 32


def _seq_layer(x, ctx0, s0T, kpre, vpre, rope_tabs, lw, fnw, final, chunk, need_out=True):
    B, L, D = x.shape
    xbc_c, dt, dtT, q, k, v, ckv, kr, conv_state = _seq_inproj(x, ctx0, rope_tabs, lw, SEQ_TILE)
    y_ssd, sT = _ssd_seq(xbc_c, dt, dtT, s0T, lw, chunk)
    if not need_out:
        return None, (ckv, kr, sT, conv_state, k, v)
    y_mla = _attn_seq(q, k, v, kpre, vpre, ATTN_TILE)
    out = _merge_rows(x.reshape(B * L, D), y_ssd.reshape(B * L, SSD_INNER), y_mla.reshape(B * L, MLA_INNER),
                      lw, fnw, final, MERGE_TILE).reshape(B, L, D)
    return out, (ckv, kr, sT, conv_state, k, v)


def kernel(x_prompt, x_sample, cache_ckv, cache_kr, state_ssm, state_conv, page_table, meta_tokens,
           norm_w, w_in, conv_w, conv_b, dt_bias, a_log, d_skip, ssd_norm_w, q_norm_w, w_uq, kv_norm_w,
           w_uk, w_uv, w_proj_ssd, w_proj_mla, w_out, final_norm_w):
    p = {'norm_w': norm_w, 'w_in': w_in, 'conv_w': conv_w, 'conv_b': conv_b, 'dt_bias': dt_bias,
         'a_log': a_log, 'd_skip': d_skip, 'ssd_norm_w': ssd_norm_w, 'q_norm_w': q_norm_w, 'w_uq': w_uq,
         'kv_norm_w': kv_norm_w, 'w_uk': w_uk, 'w_uv': w_uv, 'w_proj_ssd': w_proj_ssd,
         'w_proj_mla': w_proj_mla, 'w_out': w_out}
    depth = w_in.shape[0]
    B, L, D = x_prompt.shape
    R = x_sample.shape[0]
    n_meta = meta_tokens.shape[0]
    past_len = page_table.shape[1] * cache_ckv.shape[2]
    fnw = final_norm_w[None]

    tabs_meta = _rope_tables(jnp.arange(n_meta), ROPE_LANE0)
    tabs_prompt = _rope_tables(n_meta + jnp.arange(L), ROPE_LANE0)
    tabs_sample = _rope_tables(past_len + jnp.arange(1), ROPE_LANE0)

    hm = meta_tokens[None].astype(x_prompt.dtype)
    hp = x_prompt
    hs = x_sample.reshape(R, D)
    outs = {k: [] for k in ('ckv_p', 'kr_p', 'ssm_p', 'conv_p', 'ckv_s', 'kr_s', 'ssm_s', 'conv_s')}
    for l in range(depth):
        lw = _layer_weights(p, l)
        final = l == depth - 1
        zero_ctx = jnp.zeros((SSD_CONV - 1, SSD_CONV_DIM), F32)
        zero_state = jnp.zeros((SSD_STATE, SSD_INNER), F32)
        hm_next, (ckv_m, kr_m, sT_m, cs_m, k_m, v_m) = _seq_layer(
            hm, zero_ctx, zero_state, None, None, tabs_meta, lw, fnw, False, n_meta, need_out=not final)
        hp, (ckv_pp, kr_pp, sT_p, cs_p, _, _) = _seq_layer(
            hp, cs_m[0], sT_m[0], k_m[0], v_m[0], tabs_prompt, lw, fnw, final, SSD_CHUNK)
        hm = hm_next
        outs['ckv_p'].append(jnp.concatenate([jnp.broadcast_to(ckv_m, (B,) + ckv_m.shape[1:]), ckv_pp], 1))
        outs['kr_p'].append(jnp.concatenate([jnp.broadcast_to(kr_m, (B,) + kr_m.shape[1:]), kr_pp], 1))
        outs['ssm_p'].append(jnp.transpose(sT_p.reshape(B, SSD_STATE, SSD_HEADS, SSD_HEAD_DIM), (0, 2, 3, 1)))
        outs['conv_p'].append(cs_p)

        ctx = jnp.swapaxes(state_conv[l], 0, 1)
        xbc_c, dt, cs_s, qlat, qr, ckv_s, kr_s = _sample_inproj(hs, ctx, tabs_sample, lw)
        y_ssd, s_new = _ssd_step(xbc_c, dt, state_ssm[l].reshape(R, SSD_INNER, SSD_STATE), lw, STEP_ROWS)
        o_lat = _decode_attn(page_table, qlat, qr, ckv_s, kr_s, cache_ckv, cache_kr, l, PAGES_PER_STEP)
        y_mla = _uv_proj(o_lat, lw['w_uv_h'])
        hs = _merge_rows(hs, y_ssd, y_mla, lw, fnw, final, MERGE_TILE)
        outs['ckv_s'].append(ckv_s.reshape(R, 1, MLA_KV_LORA))
        outs['kr_s'].append(kr_s.reshape(R, 1, MLA_ROPE))
        outs['ssm_s'].append(s_new.reshape(R, SSD_HEADS, SSD_HEAD_DIM, SSD_STATE))
        outs['conv_s'].append(jnp.swapaxes(cs_s, 0, 1))

    st = lambda k: jnp.stack(outs[k])
    return (hp, hs.reshape(R, 1, D), st('ckv_p'), st('kr_p'), st('ssm_p'), st('conv_p'),
            st('ckv_s'), st('kr_s'), st('ssm_s'), st('conv_s'))
```

```python
import functools
import math

import jax
import jax.numpy as jnp
from jax import lax
from jax.experimental import pallas as pl
from jax.experimental.pallas import tpu as pltpu

N_META = 16
NORM_EPS = 1e-6
SSD_HEADS = 16
SSD_HEAD_DIM = 64
SSD_INNER = SSD_HEADS * SSD_HEAD_DIM
SSD_GROUPS = 4
HEADS_PER_GROUP = SSD_HEADS // SSD_GROUPS
SSD_STATE = 128
SSD_CONV = 4
SSD_CHUNK = 128
SSD_BC = SSD_GROUPS * SSD_STATE
SSD_CONV_DIM = SSD_INNER + 2 * SSD_BC
MLA_HEADS = 8
MLA_NOPE = 64
MLA_ROPE = 32
MLA_V = 64
MLA_Q_LORA = 384
MLA_KV_LORA = 256
MLA_INNER = MLA_HEADS * MLA_V
MLA_SCALE = (MLA_NOPE + MLA_ROPE) ** -0.5
ROPE_BASE = 10000.0
LOG2E = math.log2(math.e)

LANES = 128
HEAD_PAD = LANES
ROPE_LANE0 = MLA_NOPE
ROPE_HALF = MLA_ROPE // 2
CTX_ROW0 = 8 - (SSD_CONV - 1)
VMEM_LIMIT = 56 * 1024 * 1024

BF16 = jnp.bfloat16
F32 = jnp.float32


def _cparams(sem):
    return pltpu.CompilerParams(dimension_semantics=sem, vmem_limit_bytes=VMEM_LIMIT)


def _full(shape):
    n = len(shape)
    return pl.BlockSpec(shape, lambda *_: (0,) * n)


def _rms(x, w):
    return x * lax.rsqrt(jnp.mean(x * x, -1, keepdims=True) + NORM_EPS) * w


def _sigmoid(x):
    return 0.5 + 0.5 * jnp.tanh(0.5 * x)


def _silu(x):
    return x * _sigmoid(x)


def _softplus(x):
    return jnp.maximum(x, 0.0) + jnp.log(1.0 + jnp.exp(-jnp.abs(x)))


def _dot(a, b):
    return jnp.dot(a, b, preferred_element_type=F32)


def _dot_nt(a, b):
    return lax.dot_general(a, b, (((1,), (1,)), ((), ())), preferred_element_type=F32)


def _dot_tn(a, b):
    return lax.dot_general(a, b, (((0,), (0,)), ((), ())), preferred_element_type=F32)


def _heads(x):
    return jnp.stack([x[:, hh * HEAD_PAD:(hh + 1) * HEAD_PAD] for hh in range(MLA_HEADS)])


def _bdot_nt(a, b):
    return lax.dot_general(a, b, (((2,), (2,)), ((0,), (0,))), preferred_element_type=F32)


def _bdot(a, b):
    return lax.dot_general(a, b, (((2,), (1,)), ((0,), (0,))), preferred_element_type=F32)


def _split3(a):
    a1 = a.astype(BF16)
    r = a - a1.astype(F32)
    a2 = r.astype(BF16)
    a3 = (r - a2.astype(F32)).astype(BF16)
    return a1, a2, a3


def _dot_sel(a, sel):
    a1, a2, a3 = _split3(a)
    return _dot(a1, sel) + _dot(a2, sel) + _dot(a3, sel)


def _dot_sel2(a, sel):
    a1 = a.astype(BF16)
    a2 = (a - a1.astype(F32)).astype(BF16)
    return _dot(a1, sel) + _dot(a2, sel)


def _sel_dot(sel, a):
    a1, a2, a3 = _split3(a)
    return _dot(sel, a1) + _dot(sel, a2) + _dot(sel, a3)


def _head_expand():
    r = lax.broadcasted_iota(jnp.int32, (SSD_HEADS, SSD_INNER), 0)
    c = lax.broadcasted_iota(jnp.int32, (SSD_HEADS, SSD_INNER), 1)
    return jnp.where(c // SSD_HEAD_DIM == r, 1.0, 0.0).astype(BF16)


def _rope_lanes(x, c, s1, s2):
    n = x.shape[-1]
    return x * c + pltpu.roll(x, n - ROPE_HALF, 1) * s1 + pltpu.roll(x, ROPE_HALF, 1) * s2


def _seq_inproj_kernel(x_ref, normw_ref, wxbc_ref, convw_ref, convb_ref, ctx0_ref,
                       wdt_ref, dtb_ref, wdtT_ref, dtbT_ref,
                       wcq_ref, qnw_ref, wuq_ref, wckv_ref, kvnw_ref, wuk_ref, wuv_ref, vone_ref, wkr_ref,
                       rc_ref, rs1_ref, rs2_ref,
                       xbc_out, dt_out, dtT_out, q_out, k_out, v_out, ckv_out, kr_out, cs_out,
                       cbuf):
    t = pl.program_id(1)
    T = x_ref.shape[1]
    nctx = SSD_CONV - 1
    h = _rms(x_ref[0], normw_ref[...]).astype(BF16)

    @pl.when(t == 0)
    def _():
        cbuf[CTX_ROW0:8, :] = ctx0_ref[...]

    cbuf[8:8 + T, :] = _dot(h, wxbc_ref[...])
    acc = convb_ref[...]
    for k in range(SSD_CONV):
        acc = acc + cbuf[CTX_ROW0 + k:CTX_ROW0 + k + T, :] * convw_ref[k:k + 1, :]
    xbc_out[0] = _silu(acc)
    last = cbuf[8 + T - nctx:8 + T, :]
    cs_out[0] = last
    cbuf[CTX_ROW0:8, :] = last

    dt_out[0] = _softplus(_dot(h, wdt_ref[...]) + dtb_ref[...])
    dtT_out[0] = _softplus(_dot_nt(wdtT_ref[...], h) + dtbT_ref[...])

    rc, rs1, rs2 = rc_ref[...], rs1_ref[...], rs2_ref[...]
    cq = _rms(_dot(h, wcq_ref[...]), qnw_ref[...]).astype(BF16)
    q = _dot(cq, wuq_ref[...])
    q = _rope_lanes(q, jnp.tile(rc, (1, MLA_HEADS)), jnp.tile(rs1, (1, MLA_HEADS)),
                    jnp.tile(rs2, (1, MLA_HEADS)))
    q_out[0] = (q * (MLA_SCALE * LOG2E)).astype(BF16)

    kr = _rope_lanes(_dot(h, wkr_ref[...]), rc, rs1, rs2)
    kr_out[0] = kr[:, ROPE_LANE0:ROPE_LANE0 + MLA_ROPE]
    ckv = _rms(_dot(h, wckv_ref[...]), kvnw_ref[...])
    ckv_out[0] = ckv
    cb = ckv.astype(BF16)
    k_out[0] = (_dot(cb, wuk_ref[...]) + jnp.tile(kr, (1, MLA_HEADS))).astype(BF16)
    v_out[0] = (_dot(cb, wuv_ref[...]) + vone_ref[...]).astype(BF16)


def _seq_inproj(x, ctx0, rope_tabs, lw, tile):
    B, L, D = x.shape
    T = min(tile, L)
    rc, rs1, rs2 = rope_tabs
    row = lambda w: pl.BlockSpec((1, T, w), lambda b, t: (b, t, 0))
    tab = pl.BlockSpec((T, LANES), lambda b, t: (t, 0))
    weights = [lw['norm_w'], lw['w_xbc'], lw['conv_w'], lw['conv_b'], ctx0,
               lw['w_dt'], lw['dt_bias'], lw['w_dtT'], lw['dt_biasT'],
               lw['w_cq'], lw['q_norm_w'], lw['w_uq_p'], lw['w_ckv'], lw['kv_norm_w'],
               lw['w_uk_p'], lw['w_uv_p'], lw['v_ones'], lw['w_kr_p']]
    out_shape = (
        jax.ShapeDtypeStruct((B, L, SSD_CONV_DIM), F32),
        jax.ShapeDtypeStruct((B, L, SSD_HEADS), F32),
        jax.ShapeDtypeStruct((B, SSD_HEADS, L), F32),
        jax.ShapeDtypeStruct((B, L, MLA_HEADS * HEAD_PAD), BF16),
        jax.ShapeDtypeStruct((B, L, MLA_HEADS * HEAD_PAD), BF16),
        jax.ShapeDtypeStruct((B, L, MLA_HEADS * HEAD_PAD), BF16),
        jax.ShapeDtypeStruct((B, L, MLA_KV_LORA), F32),
        jax.ShapeDtypeStruct((B, L, MLA_ROPE), F32),
        jax.ShapeDtypeStruct((B, SSD_CONV - 1, SSD_CONV_DIM), F32),
    )
    out_specs = (row(SSD_CONV_DIM), row(SSD_HEADS),
                 pl.BlockSpec((1, SSD_HEADS, T), lambda b, t: (b, 0, t)),
                 row(MLA_HEADS * HEAD_PAD), row(MLA_HEADS * HEAD_PAD), row(MLA_HEADS * HEAD_PAD),
                 row(MLA_KV_LORA), row(MLA_ROPE),
                 pl.BlockSpec((1, SSD_CONV - 1, SSD_CONV_DIM), lambda b, t: (b, 0, 0)))
    return pl.pallas_call(
        _seq_inproj_kernel,
        grid=(B, L // T),
        in_specs=[row(D)] + [_full(w.shape) for w in weights] + [tab, tab, tab],
        out_specs=out_specs,
        out_shape=out_shape,
        scratch_shapes=[pltpu.VMEM((8 + T, SSD_CONV_DIM), F32)],
        compiler_params=_cparams(("parallel", "arbitrary")),
    )(x, *weights, rc, rs1, rs2)


def _ssd_seq_kernel(xbc_ref, dtall_ref, dtTall_ref, dtT_ref, alog_ref, alogT_ref, dskip_ref, s0_ref,
                    y_out, sT_out, sT, acs_s, acsT_s, dtw_s, cdec_s):
    c = pl.program_id(1)
    Q = xbc_ref.shape[1]
    NCH = dtall_ref.shape[1] // Q
    GW = HEADS_PER_GROUP * SSD_HEAD_DIM
    ri = lax.broadcasted_iota(jnp.int32, (Q, Q), 0)
    ci = lax.broadcasted_iota(jnp.int32, (Q, Q), 1)
    causal = ci <= ri

    @pl.when(c == 0)
    def _():
        sT[...] = s0_ref[...]
        tril = jnp.broadcast_to(jnp.where(causal, 1.0, 0.0).astype(BF16)[None], (NCH, Q, Q))
        triu = jnp.broadcast_to(jnp.where(ri <= ci, 1.0, 0.0).astype(BF16)[None], (NCH, Q, Q))
        dt3 = dtall_ref[0].reshape(NCH, Q, SSD_HEADS)
        a1, a2, a3 = _split3(dt3 * (-jnp.exp(alog_ref[...])))
        acs3 = _bdot(tril, a1) + _bdot(tril, a2) + _bdot(tril, a3)
        aT = dtTall_ref[0] * (-jnp.exp(alogT_ref[...]))
        b1, b2, b3 = _split3(jnp.stack([aT[:, k * Q:(k + 1) * Q] for k in range(NCH)]))
        acs_s[...] = acs3
        acsT_s[...] = _bdot(b1, triu) + _bdot(b2, triu) + _bdot(b3, triu)
        end3 = acs3[:, Q - 1:Q, :]
        dtw_s[...] = dt3 * jnp.exp(end3 - acs3)
        cdec_s[...] = jnp.exp(end3)

    acs = acs_s[c]
    acsT = acsT_s[c]
    dtT = dtT_ref[0]
    expand = _head_expand()
    dtw_e = _dot_sel2(dtw_s[c], expand)
    chunk_decay = _dot_sel(cdec_s[c], expand)
    xs = xbc_ref[0, :, 0:SSD_INNER]
    xs_b = xs.astype(BF16)
    xw = (xs * dtw_e).astype(BF16)
    dxs = xs * dskip_ref[...]
    lane = lax.broadcasted_iota(jnp.int32, (Q, LANES), 1)

    for g in range(SSD_GROUPS):
        lo = g * GW
        Bg = xbc_ref[0, :, SSD_INNER + g * SSD_STATE:SSD_INNER + (g + 1) * SSD_STATE].astype(BF16)
        Cg_f = xbc_ref[0, :, SSD_INNER + SSD_BC + g * SSD_STATE:SSD_INNER + SSD_BC + (g + 1) * SSD_STATE]
        cb = _dot_nt(Cg_f.astype(BF16), Bg)
        heads = range(g * HEADS_PER_GROUP, (g + 1) * HEADS_PER_GROUP)
        acs_b = jnp.stack([jnp.broadcast_to(acs[:, hh:hh + 1], (Q, LANES)) for hh in heads])
        acs_r = jnp.stack([acsT[hh:hh + 1, :] for hh in heads])
        dt_r = jnp.stack([dtT[hh:hh + 1, :] for hh in heads])
        m = cb[None] * jnp.exp(jnp.where(causal[None], acs_b[:, :, :Q] - acs_r, -jnp.inf)) * dt_r
        lhs = jnp.concatenate([Cg_f[None] * jnp.exp(acs_b), m], axis=2).astype(BF16)
        pair_rhs = [jnp.concatenate([sT[:, lo + pr * LANES:lo + (pr + 1) * LANES].astype(BF16),
                                     xs_b[:, lo + pr * LANES:lo + (pr + 1) * LANES]], axis=0)
                    for pr in range(HEADS_PER_GROUP // 2)]
        y = _bdot(lhs, jnp.stack([pair_rhs[e // 2] for e in range(HEADS_PER_GROUP)]))
        for pr in range(HEADS_PER_GROUP // 2):
            plo = lo + pr * LANES
            y_out[0, :, plo:plo + LANES] = (jnp.where(lane < SSD_HEAD_DIM, y[2 * pr], y[2 * pr + 1])
                                            + dxs[:, plo:plo + LANES])
        sT[:, lo:lo + GW] = sT[:, lo:lo + GW] * chunk_decay[:, lo:lo + GW] + _dot_tn(Bg, xw[:, lo:lo + GW])

    sT_out[0] = sT[...]


def _ssd_seq(xbc_c, dt, dtT, s0T, lw, chunk):
    B, L, _ = xbc_c.shape
    Q = min(chunk, L)
    return pl.pallas_call(
        _ssd_seq_kernel,
        grid=(B, L // Q),
        in_specs=[pl.BlockSpec((1, Q, SSD_CONV_DIM), lambda b, c: (b, c, 0)),
                  pl.BlockSpec((1, L, SSD_HEADS), lambda b, c: (b, 0, 0)),
                  pl.BlockSpec((1, SSD_HEADS, L), lambda b, c: (b, 0, 0)),
                  pl.BlockSpec((1, SSD_HEADS, Q), lambda b, c: (b, 0, c)),
                  _full((1, SSD_HEADS)), _full((SSD_HEADS, 1)), _full((1, SSD_INNER)),
                  _full((SSD_STATE, SSD_INNER))],
        out_specs=(pl.BlockSpec((1, Q, SSD_INNER), lambda b, c: (b, c, 0)),
                   pl.BlockSpec((1, SSD_STATE, SSD_INNER), lambda b, c: (b, 0, 0))),
        out_shape=(jax.ShapeDtypeStruct((B, L, SSD_INNER), F32),
                   jax.ShapeDtypeStruct((B, SSD_STATE, SSD_INNER), F32)),
        scratch_shapes=[pltpu.VMEM((SSD_STATE, SSD_INNER), F32),
                        pltpu.VMEM((L // Q, Q, SSD_HEADS), F32),
                        pltpu.VMEM((L // Q, SSD_HEADS, Q), F32),
                        pltpu.VMEM((L // Q, Q, SSD_HEADS), F32),
                        pltpu.VMEM((L // Q, 1, SSD_HEADS), F32)],
        compiler_params=_cparams(("parallel", "arbitrary")),
    )(xbc_c, dt, dtT, dtT, lw['a_log'], lw['a_logT'], lw['d_skip_e'], s0T)


def _attn_seq_kernel(*refs, n_pre):
    if n_pre:
        q_ref, k_ref, v_ref, kpre_ref, vpre_ref, o_ref, m_s, acc_s = refs
    else:
        q_ref, k_ref, v_ref, o_ref, m_s, acc_s = refs
    i = pl.program_id(1)
    TQ = q_ref.shape[1]
    causal = (lax.broadcasted_iota(jnp.int32, (TQ, TQ), 1)
              <= lax.broadcasted_iota(jnp.int32, (TQ, TQ), 0))[None]
    mshape = (MLA_HEADS, TQ, HEAD_PAD)

    def lanes_of(m, n):
        return m[..., :n] if n <= HEAD_PAD else jnp.concatenate([m] * (n // HEAD_PAD), -1)

    if n_pre:
        s = _bdot_nt(_heads(q_ref[0]), _heads(kpre_ref[...]))
        m = jnp.broadcast_to(jnp.max(s, -1, keepdims=True), mshape)
        m_s[...] = m
        acc_s[...] = _bdot(jnp.exp2(s - lanes_of(m, n_pre)).astype(BF16), _heads(vpre_ref[...]))
    else:
        m_s[...] = jnp.full(mshape, -jnp.inf, F32)
        acc_s[...] = jnp.zeros(mshape, F32)

    def step(r, masked):
        s = _bdot_nt(_heads(q_ref[0]), _heads(k_ref[0, pl.ds(r, TQ), :]))
        if masked:
            s = jnp.where(causal, s, -jnp.inf)
        m_old = m_s[...]
        m_new = jnp.maximum(m_old, jnp.max(s, -1, keepdims=True))
        p = jnp.exp2(s - lanes_of(m_new, TQ)).astype(BF16)
        acc_s[...] = acc_s[...] * jnp.exp2(m_old - m_new) + _bdot(p, _heads(v_ref[0, pl.ds(r, TQ), :]))
        m_s[...] = m_new

    def body(j, carry):
        step(pl.multiple_of(j * TQ, TQ), False)
        return carry

    lax.fori_loop(0, i, body, 0)
    step(pl.multiple_of(i * TQ, TQ), True)

    lane = lax.broadcasted_iota(jnp.int32, (TQ, HEAD_PAD), 1)
    for hp in range(MLA_HEADS // 2):
        a0, a1 = acc_s[2 * hp], acc_s[2 * hp + 1]
        o0 = a0 * (1.0 / a0[:, MLA_V:MLA_V + 1])
        o1 = a1 * (1.0 / a1[:, 0:1])
        o_ref[0, :, hp * HEAD_PAD:(hp + 1) * HEAD_PAD] = jnp.where(lane < MLA_V, o0, o1)


def _attn_seq(q, k, v, kpre, vpre, tile):
    B, L, W = q.shape
    TQ = min(tile, L)
    n_pre = 0 if kpre is None else kpre.shape[0]
    seq = pl.BlockSpec((1, L, W), lambda b, i: (b, 0, 0))
    in_specs = [pl.BlockSpec((1, TQ, W), lambda b, i: (b, i, 0)), seq, seq]
    args = [q, k, v]
    if n_pre:
        in_specs += [_full(kpre.shape), _full(vpre.shape)]
        args += [kpre, vpre]
    return pl.pallas_call(
        functools.partial(_attn_seq_kernel, n_pre=n_pre),
        grid=(B, L // TQ),
        in_specs=in_specs,
        out_specs=pl.BlockSpec((1, TQ, MLA_INNER), lambda b, i: (b, i, 0)),
        out_shape=jax.ShapeDtypeStruct((B, L, MLA_INNER), F32),
        scratch_shapes=[pltpu.VMEM((MLA_HEADS, TQ, HEAD_PAD), F32),
                        pltpu.VMEM((MLA_HEADS, TQ, HEAD_PAD), F32)],
        compiler_params=_cparams(("parallel", "arbitrary")),
    )(*args)


def _merge_kernel(x_ref, yssd_ref, ymla_ref, normw_ref, wz_ref, wg_ref, wgate_ref, ssdnw_ref,
                  wps_ref, wpm_ref, wo_ref, fnw_ref, o_ref, *, final):
    x = x_ref[...]
    h = _rms(x, normw_ref[...]).astype(BF16)
    z = _dot(h, wz_ref[...])
    y_ssd = _rms(yssd_ref[...] * _silu(z), ssdnw_ref[...]).astype(BF16)
    y_mla = (ymla_ref[...] * _silu(_dot(h, wg_ref[...]))).astype(BF16)
    gate = _sigmoid(_dot(h, wgate_ref[...]))
    D = x.shape[-1]
    u = gate[:, :D] * _dot(y_ssd, wps_ref[...]) + gate[:, D:] * _dot(y_mla, wpm_ref[...])
    out = x + _dot(u.astype(BF16), wo_ref[...])
    if final:
        out = _rms(out, fnw_ref[...])
    o_ref[...] = out


def _merge_rows(x, y_ssd, y_mla, lw, fnw, final, tile):
    R, D = x.shape
    T = min(tile, R)
    row = lambda w: pl.BlockSpec((T, w), lambda r: (r, 0))
    weights = [lw['norm_w'], lw['w_z'], lw['w_g'], lw['w_gate'], lw['ssd_norm_w'],
               lw['w_proj_ssd'], lw['w_proj_mla'], lw['w_out'], fnw]
    return pl.pallas_call(
        functools.partial(_merge_kernel, final=final),
        grid=(R // T,),
        in_specs=[row(D), row(SSD_INNER), row(MLA_INNER)] + [_full(w.shape) for w in weights],
        out_specs=row(D),
        out_shape=jax.ShapeDtypeStruct((R, D), F32),
        compiler_params=_cparams(("parallel",)),
    )(x, y_ssd, y_mla, *weights)


def _sample_inproj_kernel(x_ref, normw_ref, wxbc_ref, convw_ref, convb_ref, ctx_ref,
                          wdt_ref, dtb_ref, wcq_ref, qnw_ref, wuqn_ref, wuqr_ref, wukT_ref,
                          wckv_ref, kvnw_ref, wkr_ref, rc_ref, rs1_ref, rs2_ref,
                          xbc_out, dt_out, cs_out, qlat_out, qr_out, ckv_out, kr_out):
    C = SSD_CONV_DIM
    h = _rms(x_ref[...], normw_ref[...]).astype(BF16)
    raw = _dot(h, wxbc_ref[...])
    acc = convb_ref[...] + raw * convw_ref[SSD_CONV - 1:SSD_CONV, :]
    for k in range(SSD_CONV - 1):
        acc = acc + ctx_ref[k] * convw_ref[k:k + 1, :]
    xbc_out[...] = _silu(acc)
    for k in range(SSD_CONV - 2):
        cs_out[k] = ctx_ref[k + 1]
    cs_out[SSD_CONV - 2] = raw
    dt_out[...] = _softplus(_dot(h, wdt_ref[...]) + dtb_ref[...])

    rc, rs1, rs2 = rc_ref[...], rs1_ref[...], rs2_ref[...]
    cq = _rms(_dot(h, wcq_ref[...]), qnw_ref[...]).astype(BF16)
    qscale = MLA_SCALE * LOG2E
    qn = (_dot(cq, wuqn_ref[...]) * qscale).astype(BF16)
    qr = _rope_lanes(_dot(cq, wuqr_ref[...]), jnp.tile(rc, (1, MLA_HEADS)),
                     jnp.tile(rs1, (1, MLA_HEADS)), jnp.tile(rs2, (1, MLA_HEADS))) * qscale
    for hh in range(MLA_HEADS):
        qlat_out[:, hh, :] = _dot(qn[:, hh * MLA_NOPE:(hh + 1) * MLA_NOPE], wukT_ref[hh])
        qr_out[:, hh, :] = qr[:, hh * LANES + ROPE_LANE0:hh * LANES + ROPE_LANE0 + MLA_ROPE]
    kr = _rope_lanes(_dot(h, wkr_ref[...]), rc, rs1, rs2)
    kr_out[...] = kr[:, ROPE_LANE0:ROPE_LANE0 + MLA_ROPE]
    ckv_out[...] = _rms(_dot(h, wckv_ref[...]), kvnw_ref[...])


def _sample_inproj(x, ctx, rope_tabs, lw):
    R, D = x.shape
    rc, rs1, rs2 = rope_tabs
    args = [x, lw['norm_w'], lw['w_xbc'], lw['conv_w'], lw['conv_b'], ctx, lw['w_dt'], lw['dt_bias'],
            lw['w_cq'], lw['q_norm_w'], lw['w_uq_nope'], lw['w_uq_rope_p'], lw['w_ukT'],
            lw['w_ckv'], lw['kv_norm_w'], lw['w_kr_p'], rc, rs1, rs2]
    out_shape = (jax.ShapeDtypeStruct((R, SSD_CONV_DIM), F32),
                 jax.ShapeDtypeStruct((R, SSD_HEADS), F32),
                 jax.ShapeDtypeStruct((SSD_CONV - 1, R, SSD_CONV_DIM), F32),
                 jax.ShapeDtypeStruct((R, MLA_HEADS, MLA_KV_LORA), F32),
                 jax.ShapeDtypeStruct((R, MLA_HEADS, MLA_ROPE), F32),
                 jax.ShapeDtypeStruct((R, MLA_KV_LORA), F32),
                 jax.ShapeDtypeStruct((R, MLA_ROPE), F32))
    return pl.pallas_call(
        _sample_inproj_kernel,
        grid=(1,),
        in_specs=[_full(a.shape) for a in args],
        out_specs=tuple(_full(s.shape) for s in out_shape),
        out_shape=out_shape,
        compiler_params=_cparams(("arbitrary",)),
    )(*args)


def _ssd_step_kernel(xbc_ref, dt_ref, alog_ref, dskip_ref, s_ref, y_out, s_out):
    NB = xbc_ref.shape[0]
    GW = HEADS_PER_GROUP * SSD_HEAD_DIM
    expand = _head_expand()
    dt_e = _dot_sel(dt_ref[...], expand)
    a_e = _dot_sel(-jnp.exp(alog_ref[...]), expand)
    xs = xbc_ref[:, 0:SSD_INNER]
    xdt = xs * dt_e
    dec = jnp.exp(dt_e * a_e)
    dxs = xs * dskip_ref[...]
    KR = 16
    row = lax.broadcasted_iota(jnp.int32, (KR, SSD_INNER), 0)
    col = lax.broadcasted_iota(jnp.int32, (KR, SSD_INNER), 1)
    gmask = jnp.where((row < 2 * SSD_GROUPS) & (col // GW == row % SSD_GROUPS), 1.0, 0.0)
    rown = lax.broadcasted_iota(jnp.int32, (KR, SSD_STATE), 0)
    dec_rows = jnp.where((rown >= 2 * SSD_GROUPS) & (rown < 2 * SSD_GROUPS + 3), 1.0, 0.0)
    for b in range(NB):
        xb = xdt[b:b + 1, :]
        x1 = xb.astype(BF16).astype(F32)
        d1, d2, d3 = [d.astype(F32) for d in _split3(dec[b:b + 1, :])]
        lhs = jnp.where(row < SSD_GROUPS, x1, xb - x1) * gmask
        lhs = lhs + jnp.where(row == 2 * SSD_GROUPS, d1,
                              jnp.where(row == 2 * SSD_GROUPS + 1, d2,
                                        jnp.where(row == 2 * SSD_GROUPS + 2, d3, 0.0)))
        bmat = jnp.zeros((KR, SSD_STATE), F32)
        cmat = jnp.zeros((KR, SSD_STATE), F32)
        for g in range(SSD_GROUPS):
            bg = xbc_ref[b:b + 1, SSD_INNER + g * SSD_STATE:SSD_INNER + (g + 1) * SSD_STATE]
            cg = xbc_ref[b:b + 1, SSD_INNER + SSD_BC + g * SSD_STATE:SSD_INNER + SSD_BC + (g + 1) * SSD_STATE]
            bmat = bmat + jnp.where((rown < 2 * SSD_GROUPS) & (rown % SSD_GROUPS == g), bg, 0.0)
            cmat = cmat + jnp.where(rown == g, cg, 0.0)
        rhs = jnp.concatenate([bmat, dec_rows], axis=1).astype(BF16)
        r = _dot_tn(lhs.astype(BF16), rhs)
        s_new = s_ref[b] * r[:, SSD_STATE:] + r[:, :SSD_STATE]
        s_out[b] = s_new
        yt = _dot_nt(cmat.astype(BF16), s_new.astype(BF16))
        y_out[b:b + 1, :] = jnp.sum(yt * gmask, axis=0, keepdims=True) + dxs[b:b + 1, :]


def _ssd_step(xbc_c, dt, state, lw, nb):
    R = xbc_c.shape[0]
    return pl.pallas_call(
        _ssd_step_kernel,
        grid=(R // nb,),
        in_specs=[pl.BlockSpec((nb, SSD_CONV_DIM), lambda i: (i, 0)),
                  pl.BlockSpec((nb, SSD_HEADS), lambda i: (i, 0)),
                  _full((1, SSD_HEADS)), _full((1, SSD_INNER)),
                  pl.BlockSpec((nb, SSD_INNER, SSD_STATE), lambda i: (i, 0, 0))],
        out_specs=(pl.BlockSpec((nb, SSD_INNER), lambda i: (i, 0)),
                   pl.BlockSpec((nb, SSD_INNER, SSD_STATE), lambda i: (i, 0, 0))),
        out_shape=(jax.ShapeDtypeStruct((R, SSD_INNER), F32),
                   jax.ShapeDtypeStruct((R, SSD_INNER, SSD_STATE), F32)),
        compiler_params=_cparams(("parallel",)),
    )(xbc_c, dt, lw['a_log'], lw['d_skip_e'], state)


def _decode_kernel(pt_ref, qlat_ref, qr_ref, ckvn_ref, krn_ref, cckv_hbm, ckr_hbm, o_ref,
                   ckv_buf, kr_buf, sems, m_s, l_s, acc_s, *, layer, G):
    b = pl.program_id(0)
    c = pl.program_id(1)
    NB = pl.num_programs(0)
    NC = pl.num_programs(1)
    step = b * NC + c
    n_slots = ckv_buf.shape[0]
    ahead = n_slots - 1
    slot = lax.rem(step, n_slots)

    P = cckv_hbm.shape[2]

    def start_page(chunk, g):
        sl = lax.rem(chunk, n_slots)
        page = pt_ref[lax.div(chunk, NC), lax.rem(chunk, NC) * G + g]
        r = pl.multiple_of(g * P, P)
        pltpu.make_async_copy(cckv_hbm.at[layer, page], ckv_buf.at[sl, pl.ds(r, P)], sems.at[sl, 0]).start()
        pltpu.make_async_copy(ckr_hbm.at[layer, page], kr_buf.at[sl, :, pl.ds(r, P)], sems.at[sl, 1]).start()

    @pl.when(step == 0)
    def _():
        def prime(i, carry):
            @pl.when(lax.div(i, G) < NB * NC)
            def _():
                start_page(lax.div(i, G), lax.rem(i, G))
            return carry
        lax.fori_loop(0, ahead * G, prime, 0)

    pltpu.make_async_copy(ckv_buf.at[slot], ckv_buf.at[slot], sems.at[slot, 0]).wait()
    pltpu.make_async_copy(kr_buf.at[slot], kr_buf.at[slot], sems.at[slot, 1]).wait()

    @pl.when(step + ahead < NB * NC)
    def _():
        for g in range(G):
            start_page(step + ahead, g)

    @pl.when(c == 0)
    def _():
        m_s[...] = jnp.full(m_s.shape, -jnp.inf, F32)
        l_s[...] = jnp.zeros(l_s.shape, F32)
        acc_s[...] = jnp.zeros(acc_s.shape, F32)

    ql = qlat_ref[0].astype(BF16)
    qr = qr_ref[0].astype(BF16)
    kc = ckv_buf[slot].astype(BF16)
    s = _dot_nt(ql, kc) + _dot(qr, kr_buf[slot].astype(BF16))
    m, l, acc = m_s[...], l_s[...], acc_s[...]
    n_sub = 2 if G % 2 == 0 else 1
    W = G * P // n_sub
    for u in range(n_sub):
        su = s[:, u * W:(u + 1) * W]
        m_new = jnp.maximum(m, jnp.max(su, -1, keepdims=True))
        corr = jnp.exp2(m - m_new)
        p = jnp.exp2(su - m_new)
        l = l * corr + jnp.sum(p, -1, keepdims=True)
        acc = acc * corr + _dot(p.astype(BF16), kc[u * W:(u + 1) * W])
        m = m_new
    m_s[...], l_s[...], acc_s[...] = m, l, acc

    @pl.when(c == NC - 1)
    def _():
        kn = ckvn_ref[0]
        s1 = (jnp.sum(qlat_ref[0] * kn, -1, keepdims=True)
              + jnp.sum(qr_ref[0] * krn_ref[0], -1, keepdims=True))
        m2 = jnp.maximum(m, s1)
        corr2 = jnp.exp2(m - m2)
        p1 = jnp.exp2(s1 - m2)
        o_ref[0] = (acc * corr2 + p1 * kn) * (1.0 / (l * corr2 + p1))


def _decode_attn(page_table, qlat, qr, ckv_new, kr_new, cache_ckv, cache_kr, layer, G):
    R, n_pages = page_table.shape
    page = cache_ckv.shape[2]
    G = min(G, n_pages)
    grid_spec = pltpu.PrefetchScalarGridSpec(
        num_scalar_prefetch=1,
        grid=(R, n_pages // G),
        in_specs=[pl.BlockSpec((1, MLA_HEADS, MLA_KV_LORA), lambda b, c, pt: (b, 0, 0)),
                  pl.BlockSpec((1, MLA_HEADS, MLA_ROPE), lambda b, c, pt: (b, 0, 0)),
                  pl.BlockSpec((1, 1, MLA_KV_LORA), lambda b, c, pt: (b, 0, 0)),
                  pl.BlockSpec((1, 1, MLA_ROPE), lambda b, c, pt: (b, 0, 0)),
                  pl.BlockSpec(memory_space=pl.ANY),
                  pl.BlockSpec(memory_space=pl.ANY)],
        out_specs=pl.BlockSpec((1, MLA_HEADS, MLA_KV_LORA), lambda b, c, pt: (b, 0, 0)),
        scratch_shapes=[pltpu.VMEM((DECODE_SLOTS, G * page, MLA_KV_LORA), F32),
                        pltpu.VMEM((DECODE_SLOTS, MLA_ROPE, G * page), F32),
                        pltpu.SemaphoreType.DMA((DECODE_SLOTS, 2)),
                        pltpu.VMEM((MLA_HEADS, 1), F32),
                        pltpu.VMEM((MLA_HEADS, 1), F32),
                        pltpu.VMEM((MLA_HEADS, MLA_KV_LORA), F32)])
    return pl.pallas_call(
        functools.partial(_decode_kernel, layer=layer, G=G),
        grid_spec=grid_spec,
        out_shape=jax.ShapeDtypeStruct((R, MLA_HEADS, MLA_KV_LORA), F32),
        compiler_params=_cparams(("arbitrary", "arbitrary")),
    )(page_table, qlat, qr, ckv_new.reshape(R, 1, MLA_KV_LORA), kr_new.reshape(R, 1, MLA_ROPE),
      cache_ckv, jnp.swapaxes(cache_kr, 2, 3))


def _uv_proj_kernel(o_ref, wuv_ref, y_ref):
    for hh in range(MLA_HEADS):
        y_ref[:, hh * MLA_V:(hh + 1) * MLA_V] = _dot(o_ref[:, hh, :].astype(BF16), wuv_ref[hh])


def _uv_proj(o_lat, w_uv_h):
    R = o_lat.shape[0]
    return pl.pallas_call(
        _uv_proj_kernel,
        grid=(1,),
        in_specs=[_full(o_lat.shape), _full(w_uv_h.shape)],
        out_specs=_full((R, MLA_INNER)),
        out_shape=jax.ShapeDtypeStruct((R, MLA_INNER), F32),
        compiler_params=_cparams(("arbitrary",)),
    )(o_lat, w_uv_h)


def _rope_tables(pos, lane0):
    inv = 1.0 / (ROPE_BASE ** (jnp.arange(ROPE_HALF, dtype=F32) / ROPE_HALF))
    ang = pos.astype(F32)[:, None] * inv[None, :]
    cos, sin = jnp.cos(ang), jnp.sin(ang)
    n = pos.shape[0]
    z = lambda w: jnp.zeros((n, w), F32)
    pre = jnp.ones((n, lane0), F32)
    tail = LANES - lane0 - MLA_ROPE
    rc = jnp.concatenate([pre, cos, cos, z(tail)], 1)
    rs1 = jnp.concatenate([z(lane0), -sin, z(ROPE_HALF), z(tail)], 1)
    rs2 = jnp.concatenate([z(lane0), z(ROPE_HALF), sin, z(tail)], 1)
    return rc, rs1, rs2


def _layer_weights(p, l):
    D = p['w_in'].shape[1]
    w_in = p['w_in'][l]
    offs = [0]
    for n in (SSD_INNER, SSD_CONV_DIM, SSD_HEADS, MLA_Q_LORA, MLA_KV_LORA, MLA_ROPE, MLA_INNER, 2 * D):
        offs.append(offs[-1] + n)
    w_z, w_xbc, w_dt, w_cq, w_ckv, w_kr, w_g, w_gate = [w_in[:, offs[i]:offs[i + 1]] for i in range(8)]
    bf = lambda a: a.astype(BF16)
    pad_head = lambda a: jnp.pad(a, ((0, 0), (0, 0), (0, HEAD_PAD - a.shape[-1])))
    w_uq = p['w_uq'][l].reshape(MLA_Q_LORA, MLA_HEADS, MLA_NOPE + MLA_ROPE)
    w_uk = p['w_uk'][l].reshape(MLA_KV_LORA, MLA_HEADS, MLA_NOPE)
    w_uv = p['w_uv'][l]
    w_uv_pairs = w_uv.reshape(MLA_KV_LORA, MLA_HEADS // 2, 2, MLA_V)
    zv = jnp.zeros((MLA_KV_LORA, MLA_HEADS // 2, MLA_V), F32)
    w_uv_p = jnp.stack([jnp.concatenate([w_uv_pairs[:, :, 0], zv], -1),
                        jnp.concatenate([zv, w_uv_pairs[:, :, 1]], -1)], 2)
    one_at = lambda lane: (jnp.arange(HEAD_PAD) == lane).astype(F32)
    v_ones = jnp.tile(jnp.concatenate([one_at(MLA_V), one_at(0)]), MLA_HEADS // 2)[None]
    w_kr_p = jnp.pad(w_kr, ((0, 0), (ROPE_LANE0, LANES - ROPE_LANE0 - MLA_ROPE)))
    w_uq_rope_p = jnp.pad(w_uq[:, :, MLA_NOPE:], ((0, 0), (0, 0), (ROPE_LANE0, LANES - ROPE_LANE0 - MLA_ROPE)))
    return {
        'norm_w': p['norm_w'][l][None], 'w_xbc': bf(w_xbc), 'conv_w': p['conv_w'][l],
        'conv_b': p['conv_b'][l][None], 'w_dt': bf(w_dt), 'dt_bias': p['dt_bias'][l][None],
        'w_dtT': bf(w_dt.T), 'dt_biasT': p['dt_bias'][l][:, None],
        'w_cq': bf(w_cq), 'q_norm_w': p['q_norm_w'][l][None],
        'w_uq_p': bf(pad_head(w_uq).reshape(MLA_Q_LORA, MLA_HEADS * HEAD_PAD)),
        'w_uq_nope': bf(w_uq[:, :, :MLA_NOPE].reshape(MLA_Q_LORA, MLA_HEADS * MLA_NOPE)),
        'w_uq_rope_p': bf(w_uq_rope_p.reshape(MLA_Q_LORA, MLA_HEADS * LANES)),
        'w_ckv': bf(w_ckv), 'kv_norm_w': p['kv_norm_w'][l][None],
        'w_uk_p': bf(pad_head(w_uk).reshape(MLA_KV_LORA, MLA_HEADS * HEAD_PAD)),
        'w_ukT': bf(jnp.transpose(w_uk, (1, 2, 0))),
        'w_uv_p': bf(w_uv_p.reshape(MLA_KV_LORA, MLA_HEADS * HEAD_PAD)), 'v_ones': v_ones,
        'w_uv_h': bf(jnp.transpose(w_uv.reshape(MLA_KV_LORA, MLA_HEADS, MLA_V), (1, 0, 2))),
        'w_kr_p': bf(w_kr_p),
        'a_log': p['a_log'][l][None], 'a_logT': p['a_log'][l][:, None],
        'd_skip_e': jnp.repeat(p['d_skip'][l], SSD_HEAD_DIM)[None],
        'w_z': bf(w_z), 'w_g': bf(w_g), 'w_gate': bf(w_gate),
        'ssd_norm_w': p['ssd_norm_w'][l][None],
        'w_proj_ssd': bf(p['w_proj_ssd'][l]), 'w_proj_mla': bf(p['w_proj_mla'][l]), 'w_out': bf(p['w_out'][l]),
    }


SEQ_TILE = 512
ATTN_TILE = 256
MERGE_TILE = 512
STEP_ROWS = 8
PAGES_PER_STEP = 32
DECODE_SLOTS = 3


def _seq_layer(x, ctx0, s0T, kpre, vpre, rope_tabs, lw, fnw, final, chunk, need_out=True):
    B, L, D = x.shape
    xbc_c, dt, dtT, q, k, v, ckv, kr, conv_state = _seq_inproj(x, ctx0, rope_tabs, lw, SEQ_TILE)
    y_ssd, sT = _ssd_seq(xbc_c, dt, dtT, s0T, lw, chunk)
    if not need_out:
        return None, (ckv, kr, sT, conv_state, k, v)
    y_mla = _attn_seq(q, k, v, kpre, vpre, ATTN_TILE)
    out = _merge_rows(x.reshape(B * L, D), y_ssd.reshape(B * L, SSD_INNER), y_mla.reshape(B * L, MLA_INNER),
                      lw, fnw, final, MERGE_TILE).reshape(B, L, D)
    return out, (ckv, kr, sT, conv_state, k, v)


def kernel(x_prompt, x_sample, cache_ckv, cache_kr, state_ssm, state_conv, page_table, meta_tokens,
           norm_w, w_in, conv_w, conv_b, dt_bias, a_log, d_skip, ssd_norm_w, q_norm_w, w_uq, kv_norm_w,
           w_uk, w_uv, w_proj_ssd, w_proj_mla, w_out, final_norm_w):
    p = {'norm_w': norm_w, 'w_in': w_in, 'conv_w': conv_w, 'conv_b': conv_b, 'dt_bias': dt_bias,
         'a_log': a_log, 'd_skip': d_skip, 'ssd_norm_w': ssd_norm_w, 'q_norm_w': q_norm_w, 'w_uq': w_uq,
         'kv_norm_w': kv_norm_w, 'w_uk': w_uk, 'w_uv': w_uv, 'w_proj_ssd': w_proj_ssd,
         'w_proj_mla': w_proj_mla, 'w_out': w_out}
    depth = w_in.shape[0]
    B, L, D = x_prompt.shape
    R = x_sample.shape[0]
    n_meta = meta_tokens.shape[0]
    past_len = page_table.shape[1] * cache_ckv.shape[2]
    fnw = final_norm_w[None]

    tabs_meta = _rope_tables(jnp.arange(n_meta), ROPE_LANE0)
    tabs_prompt = _rope_tables(n_meta + jnp.arange(L), ROPE_LANE0)
    tabs_sample = _rope_tables(past_len + jnp.arange(1), ROPE_LANE0)

    hm = meta_tokens[None].astype(x_prompt.dtype)
    hp = x_prompt
    hs = x_sample.reshape(R, D)
    outs = {k: [] for k in ('ckv_p', 'kr_p', 'ssm_p', 'conv_p', 'ckv_s', 'kr_s', 'ssm_s', 'conv_s')}
    for l in range(depth):
        lw = _layer_weights(p, l)
        final = l == depth - 1
        zero_ctx = jnp.zeros((SSD_CONV - 1, SSD_CONV_DIM), F32)
        zero_state = jnp.zeros((SSD_STATE, SSD_INNER), F32)
        hm_next, (ckv_m, kr_m, sT_m, cs_m, k_m, v_m) = _seq_layer(
            hm, zero_ctx, zero_state, None, None, tabs_meta, lw, fnw, False, n_meta, need_out=not final)
        hp, (ckv_pp, kr_pp, sT_p, cs_p, _, _) = _seq_layer(
            hp, cs_m[0], sT_m[0], k_m[0], v_m[0], tabs_prompt, lw, fnw, final, SSD_CHUNK)
        hm = hm_next
        outs['ckv_p'].append(jnp.concatenate([jnp.broadcast_to(ckv_m, (B,) + ckv_m.shape[1:]), ckv_pp], 1))
        outs['kr_p'].append(jnp.concatenate([jnp.broadcast_to(kr_m, (B,) + kr_m.shape[1:]), kr_pp], 1))
        outs['ssm_p'].append(jnp.transpose(sT_p.reshape(B, SSD_STATE, SSD_HEADS, SSD_HEAD_DIM), (0, 2, 3, 1)))
        outs['conv_p'].append(cs_p)

        ctx = jnp.swapaxes(state_conv[l], 0, 1)
        xbc_c, dt, cs_s, qlat, qr, ckv_s, kr_s = _sample_inproj(hs, ctx, tabs_sample, lw)
        y_ssd, s_new = _ssd_step(xbc_c, dt, state_ssm[l].reshape(R, SSD_INNER, SSD_STATE), lw, STEP_ROWS)
        o_lat = _decode_attn(page_table, qlat, qr, ckv_s, kr_s, cache_ckv, cache_kr, l, PAGES_PER_STEP)
        y_mla = _uv_proj(o_lat, lw['w_uv_h'])
        hs = _merge_rows(hs, y_ssd, y_mla, lw, fnw, final, MERGE_TILE)
        outs['ckv_s'].append(ckv_s.reshape(R, 1, MLA_KV_LORA))
        outs['kr_s'].append(kr_s.reshape(R, 1, MLA_ROPE))
        outs['ssm_s'].append(s_new.reshape(R, SSD_HEADS, SSD_HEAD_DIM, SSD_STATE))
        outs['conv_s'].append(jnp.swapaxes(cs_s, 0, 1))

    st = lambda k: jnp.stack(outs[k])
    return (hp, hs.reshape(R, 1, D), st('ckv_p'), st('kr_p'), st('ssm_p'), st('conv_p'),
            st('ckv_s'), st('kr_s'), st('ssm_s'), st('conv_s'))
```

```python
import functools
import math

import jax
import jax.numpy as jnp
from jax import lax
from jax.experimental import pallas as pl
from jax.experimental.pallas import tpu as pltpu

N_META = 16
NORM_EPS = 1e-6
SSD_HEADS = 16
SSD_HEAD_DIM = 64
SSD_INNER = SSD_HEADS * SSD_HEAD_DIM
SSD_GROUPS = 4
HEADS_PER_GROUP = SSD_HEADS // SSD_GROUPS
SSD_STATE = 128
SSD_CONV = 4
SSD_CHUNK = 128
SSD_BC = SSD_GROUPS * SSD_STATE
SSD_CONV_DIM = SSD_INNER + 2 * SSD_BC
MLA_HEADS = 8
MLA_NOPE = 64
MLA_ROPE = 32
MLA_V = 64
MLA_Q_LORA = 384
MLA_KV_LORA = 256
MLA_INNER = MLA_HEADS * MLA_V
MLA_SCALE = (MLA_NOPE + MLA_ROPE) ** -0.5
ROPE_BASE = 10000.0
LOG2E = math.log2(math.e)

LANES = 128
HEAD_PAD = LANES
ROPE_LANE0 = MLA_NOPE
ROPE_HALF = MLA_ROPE // 2
CTX_ROW0 = 8 - (SSD_CONV - 1)
VMEM_LIMIT = 56 * 1024 * 1024

BF16 = jnp.bfloat16
F32 = jnp.float32


def _cparams(sem):
    return pltpu.CompilerParams(dimension_semantics=sem, vmem_limit_bytes=VMEM_LIMIT)


def _full(shape):
    n = len(shape)
    return pl.BlockSpec(shape, lambda *_: (0,) * n)


def _rms(x, w):
    return x * lax.rsqrt(jnp.mean(x * x, -1, keepdims=True) + NORM_EPS) * w


def _sigmoid(x):
    return 0.5 + 0.5 * jnp.tanh(0.5 * x)


def _silu(x):
    return x * _sigmoid(x)


def _softplus(x):
    return jnp.maximum(x, 0.0) + jnp.log(1.0 + jnp.exp(-jnp.abs(x)))


def _dot(a, b):
    return jnp.dot(a, b, preferred_element_type=F32)


def _dot_nt(a, b):
    return lax.dot_general(a, b, (((1,), (1,)), ((), ())), preferred_element_type=F32)


def _dot_tn(a, b):
    return lax.dot_general(a, b, (((0,), (0,)), ((), ())), preferred_element_type=F32)


def _heads(x):
    return jnp.stack([x[:, hh * HEAD_PAD:(hh + 1) * HEAD_PAD] for hh in range(MLA_HEADS)])


def _bdot_nt(a, b):
    return lax.dot_general(a, b, (((2,), (2,)), ((0,), (0,))), preferred_element_type=F32)


def _bdot(a, b):
    return lax.dot_general(a, b, (((2,), (1,)), ((0,), (0,))), preferred_element_type=F32)


def _split3(a):
    a1 = a.astype(BF16)
    r = a - a1.astype(F32)
    a2 = r.astype(BF16)
    a3 = (r - a2.astype(F32)).astype(BF16)
    return a1, a2, a3


def _dot_sel(a, sel):
    a1, a2, a3 = _split3(a)
    return _dot(a1, sel) + _dot(a2, sel) + _dot(a3, sel)


def _dot_sel2(a, sel):
    a1 = a.astype(BF16)
    a2 = (a - a1.astype(F32)).astype(BF16)
    return _dot(a1, sel) + _dot(a2, sel)


def _sel_dot(sel, a):
    a1, a2, a3 = _split3(a)
    return _dot(sel, a1) + _dot(sel, a2) + _dot(sel, a3)


def _head_expand():
    r = lax.broadcasted_iota(jnp.int32, (SSD_HEADS, SSD_INNER), 0)
    c = lax.broadcasted_iota(jnp.int32, (SSD_HEADS, SSD_INNER), 1)
    return jnp.where(c // SSD_HEAD_DIM == r, 1.0, 0.0).astype(BF16)


def _rope_lanes(x, c, s1, s2):
    n = x.shape[-1]
    return x * c + pltpu.roll(x, n - ROPE_HALF, 1) * s1 + pltpu.roll(x, ROPE_HALF, 1) * s2


def _seq_inproj_kernel(x_ref, normw_ref, wxbc_ref, convw_ref, convb_ref, ctx0_ref,
                       wdt_ref, dtb_ref, wdtT_ref, dtbT_ref,
                       wcq_ref, qnw_ref, wuq_ref, wckv_ref, kvnw_ref, wuk_ref, wuv_ref, vone_ref, wkr_ref,
                       rc_ref, rs1_ref, rs2_ref,
                       xbc_out, dt_out, dtT_out, q_out, k_out, v_out, ckv_out, kr_out, cs_out,
                       cbuf):
    t = pl.program_id(1)
    T = x_ref.shape[1]
    nctx = SSD_CONV - 1
    h = _rms(x_ref[0], normw_ref[...]).astype(BF16)

    @pl.when(t == 0)
    def _():
        cbuf[CTX_ROW0:8, :] = ctx0_ref[...]

    cbuf[8:8 + T, :] = _dot(h, wxbc_ref[...])
    acc = convb_ref[...]
    for k in range(SSD_CONV):
        acc = acc + cbuf[CTX_ROW0 + k:CTX_ROW0 + k + T, :] * convw_ref[k:k + 1, :]
    xbc_out[0] = _silu(acc)
    last = cbuf[8 + T - nctx:8 + T, :]
    cs_out[0] = last
    cbuf[CTX_ROW0:8, :] = last

    dt_out[0] = _softplus(_dot(h, wdt_ref[...]) + dtb_ref[...])
    dtT_out[0] = _softplus(_dot_nt(wdtT_ref[...], h) + dtbT_ref[...])

    rc, rs1, rs2 = rc_ref[...], rs1_ref[...], rs2_ref[...]
    cq = _rms(_dot(h, wcq_ref[...]), qnw_ref[...]).astype(BF16)
    q = _dot(cq, wuq_ref[...])
    q = _rope_lanes(q, jnp.tile(rc, (1, MLA_HEADS)), jnp.tile(rs1, (1, MLA_HEADS)),
                    jnp.tile(rs2, (1, MLA_HEADS)))
    q_out[0] = (q * (MLA_SCALE * LOG2E)).astype(BF16)

    kr = _rope_lanes(_dot(h, wkr_ref[...]), rc, rs1, rs2)
    kr_out[0] = kr[:, ROPE_LANE0:ROPE_LANE0 + MLA_ROPE]
    ckv = _rms(_dot(h, wckv_ref[...]), kvnw_ref[...])
    ckv_out[0] = ckv
    cb = ckv.astype(BF16)
    k_out[0] = (_dot(cb, wuk_ref[...]) + jnp.tile(kr, (1, MLA_HEADS))).astype(BF16)
    v_out[0] = (_dot(cb, wuv_ref[...]) + vone_ref[...]).astype(BF16)


def _seq_inproj(x, ctx0, rope_tabs, lw, tile):
    B, L, D = x.shape
    T = min(tile, L)
    rc, rs1, rs2 = rope_tabs
    row = lambda w: pl.BlockSpec((1, T, w), lambda b, t: (b, t, 0))
    tab = pl.BlockSpec((T, LANES), lambda b, t: (t, 0))
    weights = [lw['norm_w'], lw['w_xbc'], lw['conv_w'], lw['conv_b'], ctx0,
               lw['w_dt'], lw['dt_bias'], lw['w_dtT'], lw['dt_biasT'],
               lw['w_cq'], lw['q_norm_w'], lw['w_uq_p'], lw['w_ckv'], lw['kv_norm_w'],
               lw['w_uk_p'], lw['w_uv_p'], lw['v_ones'], lw['w_kr_p']]
    out_shape = (
        jax.ShapeDtypeStruct((B, L, SSD_CONV_DIM), F32),
        jax.ShapeDtypeStruct((B, L, SSD_HEADS), F32),
        jax.ShapeDtypeStruct((B, SSD_HEADS, L), F32),
        jax.ShapeDtypeStruct((B, L, MLA_HEADS * HEAD_PAD), BF16),
        jax.ShapeDtypeStruct((B, L, MLA_HEADS * HEAD_PAD), BF16),
        jax.ShapeDtypeStruct((B, L, MLA_HEADS * HEAD_PAD), BF16),
        jax.ShapeDtypeStruct((B, L, MLA_KV_LORA), F32),
        jax.ShapeDtypeStruct((B, L, MLA_ROPE), F32),
        jax.ShapeDtypeStruct((B, SSD_CONV - 1, SSD_CONV_DIM), F32),
    )
    out_specs = (row(SSD_CONV_DIM), row(SSD_HEADS),
                 pl.BlockSpec((1, SSD_HEADS, T), lambda b, t: (b, 0, t)),
                 row(MLA_HEADS * HEAD_PAD), row(MLA_HEADS * HEAD_PAD), row(MLA_HEADS * HEAD_PAD),
                 row(MLA_KV_LORA), row(MLA_ROPE),
                 pl.BlockSpec((1, SSD_CONV - 1, SSD_CONV_DIM), lambda b, t: (b, 0, 0)))
    return pl.pallas_call(
        _seq_inproj_kernel,
        grid=(B, L // T),
        in_specs=[row(D)] + [_full(w.shape) for w in weights] + [tab, tab, tab],
        out_specs=out_specs,
        out_shape=out_shape,
        scratch_shapes=[pltpu.VMEM((8 + T, SSD_CONV_DIM), F32)],
        compiler_params=_cparams(("parallel", "arbitrary")),
    )(x, *weights, rc, rs1, rs2)


def _ssd_seq_kernel(xbc_ref, dtall_ref, dtTall_ref, dtT_ref, alog_ref, alogT_ref, dskip_ref, s0_ref,
                    y_out, s_out, sT, acs_s, acsT_s, dtw_s, cdec_s):
    c = pl.program_id(1)
    NCH, Q, _ = acs_s.shape
    CPS = xbc_ref.shape[1] // Q
    GW = HEADS_PER_GROUP * SSD_HEAD_DIM
    ri = lax.broadcasted_iota(jnp.int32, (Q, Q), 0)
    ci = lax.broadcasted_iota(jnp.int32, (Q, Q), 1)
    causal = ci <= ri

    @pl.when(c == 0)
    def _():
        sT[...] = s0_ref[...]
        tril = jnp.broadcast_to(jnp.where(causal, 1.0, 0.0).astype(BF16)[None], (NCH, Q, Q))
        triu = jnp.broadcast_to(jnp.where(ri <= ci, 1.0, 0.0).astype(BF16)[None], (NCH, Q, Q))
        dt3 = dtall_ref[0].reshape(NCH, Q, SSD_HEADS)
        a1, a2, a3 = _split3(dt3 * (-jnp.exp(alog_ref[...])))
        acs3 = _bdot(tril, a1) + _bdot(tril, a2) + _bdot(tril, a3)
        aT = dtTall_ref[0] * (-jnp.exp(alogT_ref[...]))
        b1, b2, b3 = _split3(jnp.stack([aT[:, k * Q:(k + 1) * Q] for k in range(NCH)]))
        acs_s[...] = acs3
        acsT_s[...] = _bdot(b1, triu) + _bdot(b2, triu) + _bdot(b3, triu)
        end3 = acs3[:, Q - 1:Q, :]
        dtw_s[...] = dt3 * jnp.exp(end3 - acs3)
        cdec_s[...] = jnp.exp(end3)

    expand = _head_expand()
    lane = lax.broadcasted_iota(jnp.int32, (Q, LANES), 1)
    for k in range(CPS):
        rows = slice(k * Q, (k + 1) * Q)
        ch = c * CPS + k
        acs = acs_s[ch]
        acsT = acsT_s[ch]
        dtT = dtT_ref[0, :, rows]
        dtw_e = _dot_sel2(dtw_s[ch], expand)
        chunk_decay = _dot_sel(cdec_s[ch], expand)
        xs = xbc_ref[0, rows, 0:SSD_INNER]
        xs_b = xs.astype(BF16)
        xw = (xs * dtw_e).astype(BF16)
        dxs = xs * dskip_ref[...]

        for g in range(SSD_GROUPS):
            lo = g * GW
            Bg = xbc_ref[0, rows, SSD_INNER + g * SSD_STATE:SSD_INNER + (g + 1) * SSD_STATE].astype(BF16)
            Cg_f = xbc_ref[0, rows, SSD_INNER + SSD_BC + g * SSD_STATE:SSD_INNER + SSD_BC + (g + 1) * SSD_STATE]
            cb = _dot_nt(Cg_f.astype(BF16), Bg)
            heads = range(g * HEADS_PER_GROUP, (g + 1) * HEADS_PER_GROUP)
            acs_b = jnp.stack([jnp.broadcast_to(acs[:, hh:hh + 1], (Q, LANES)) for hh in heads])
            acs_r = jnp.stack([acsT[hh:hh + 1, :] for hh in heads])
            dt_r = jnp.stack([dtT[hh:hh + 1, :] for hh in heads])
            m = cb[None] * jnp.exp(jnp.where(causal[None], acs_b[:, :, :Q] - acs_r, -jnp.inf)) * dt_r
            lhs = jnp.concatenate([Cg_f[None] * jnp.exp(acs_b), m], axis=2).astype(BF16)
            pair_rhs = [jnp.concatenate([sT[:, lo + pr * LANES:lo + (pr + 1) * LANES].astype(BF16),
                                         xs_b[:, lo + pr * LANES:lo + (pr + 1) * LANES]], axis=0)
                        for pr in range(HEADS_PER_GROUP // 2)]
            y = _bdot(lhs, jnp.stack([pair_rhs[e // 2] for e in range(HEADS_PER_GROUP)]))
            for pr in range(HEADS_PER_GROUP // 2):
                plo = lo + pr * LANES
                y_out[0, rows, plo:plo + LANES] = (jnp.where(lane < SSD_HEAD_DIM, y[2 * pr], y[2 * pr + 1])
                                                   + dxs[:, plo:plo + LANES])
            sT[:, lo:lo + GW] = sT[:, lo:lo + GW] * chunk_decay[:, lo:lo + GW] + _dot_tn(Bg, xw[:, lo:lo + GW])

    @pl.when(c == pl.num_programs(1) - 1)
    def _():
        for t in range(SSD_INNER // LANES):
            s_out[0, t * LANES:(t + 1) * LANES, :] = sT[:, t * LANES:(t + 1) * LANES].T


def _ssd_seq(xbc_c, dt, dtT, s0T, lw, chunk):
    B, L, _ = xbc_c.shape
    Q = min(chunk, L)
    R = Q * min(SSD_CHUNKS_PER_STEP, L // Q)
    return pl.pallas_call(
        _ssd_seq_kernel,
        grid=(B, L // R),
        in_specs=[pl.BlockSpec((1, R, SSD_CONV_DIM), lambda b, c: (b, c, 0)),
                  pl.BlockSpec((1, L, SSD_HEADS), lambda b, c: (b, 0, 0)),
                  pl.BlockSpec((1, SSD_HEADS, L), lambda b, c: (b, 0, 0)),
                  pl.BlockSpec((1, SSD_HEADS, R), lambda b, c: (b, 0, c)),
                  _full((1, SSD_HEADS)), _full((SSD_HEADS, 1)), _full((1, SSD_INNER)),
                  _full((SSD_STATE, SSD_INNER))],
        out_specs=(pl.BlockSpec((1, R, SSD_INNER), lambda b, c: (b, c, 0)),
                   pl.BlockSpec((1, SSD_INNER, SSD_STATE), lambda b, c: (b, 0, 0))),
        out_shape=(jax.ShapeDtypeStruct((B, L, SSD_INNER), F32),
                   jax.ShapeDtypeStruct((B, SSD_INNER, SSD_STATE), F32)),
        scratch_shapes=[pltpu.VMEM((SSD_STATE, SSD_INNER), F32),
                        pltpu.VMEM((L // Q, Q, SSD_HEADS), F32),
                        pltpu.VMEM((L // Q, SSD_HEADS, Q), F32),
                        pltpu.VMEM((L // Q, Q, SSD_HEADS), F32),
                        pltpu.VMEM((L // Q, 1, SSD_HEADS), F32)],
        compiler_params=_cparams(("parallel", "arbitrary")),
    )(xbc_c, dt, dtT, dtT, lw['a_log'], lw['a_logT'], lw['d_skip_e'], s0T)


def _attn_seq_kernel(*refs, n_pre, single_tile):
    if n_pre:
        q_ref, k_ref, v_ref, kpre_ref, vpre_ref, o_ref, m_s, acc_s = refs
    else:
        q_ref, k_ref, v_ref, o_ref, m_s, acc_s = refs
    i = pl.program_id(1)
    TQ = q_ref.shape[1]
    TKB = 2 * TQ
    causal = (lax.broadcasted_iota(jnp.int32, (TQ, TQ), 1)
              <= lax.broadcasted_iota(jnp.int32, (TQ, TQ), 0))[None]
    mshape = (MLA_HEADS, TQ, HEAD_PAD)
    m_s[...] = jnp.full(mshape, -jnp.inf, F32)
    acc_s[...] = jnp.zeros(mshape, F32)

    def lanes_of(m, n):
        return m[..., :n] if n <= HEAD_PAD else jnp.concatenate([m] * (n // HEAD_PAD), -1)

    def update(s, vs):
        m_old = m_s[...]
        m_new = jnp.maximum(m_old, jnp.max(s, -1, keepdims=True))
        p = jnp.exp2(s - lanes_of(m_new, s.shape[-1])).astype(BF16)
        acc_s[...] = acc_s[...] * jnp.exp2(m_old - m_new) + _bdot(p, vs)
        m_s[...] = m_new

    def body(j, carry):
        r = pl.multiple_of(j * TKB, TKB)
        update(_bdot_nt(_heads(q_ref[0]), _heads(k_ref[0, pl.ds(r, TKB), :])),
               _heads(v_ref[0, pl.ds(r, TKB), :]))
        return carry

    lax.fori_loop(0, lax.div(i, 2), body, 0)

    def tail(with_full):
        q3 = _heads(q_ref[0])
        ss, vv = [], []
        if n_pre:
            col = lax.broadcasted_iota(jnp.int32, (TQ, kpre_ref.shape[0]), 1)
            ss.append(jnp.where((col < n_pre)[None], _bdot_nt(q3, _heads(kpre_ref[...])), -jnp.inf))
            vv.append(_heads(vpre_ref[...]))
        r_diag = pl.multiple_of(i * TQ, TQ)
        if with_full:
            r_full = pl.multiple_of(r_diag - TQ, TQ)
            ss.append(_bdot_nt(q3, _heads(k_ref[0, pl.ds(r_full, TQ), :])))
            vv.append(_heads(v_ref[0, pl.ds(r_full, TQ), :]))
        ss.append(jnp.where(causal, _bdot_nt(q3, _heads(k_ref[0, pl.ds(r_diag, TQ), :])), -jnp.inf))
        vv.append(_heads(v_ref[0, pl.ds(r_diag, TQ), :]))
        update(jnp.concatenate(ss, -1), jnp.concatenate(vv, 1))

    if single_tile:
        tail(False)
    else:
        pl.when(lax.rem(i, 2) == 0)(lambda: tail(False))
        pl.when(lax.rem(i, 2) == 1)(lambda: tail(True))

    lane = lax.broadcasted_iota(jnp.int32, (TQ, HEAD_PAD), 1)
    for hp in range(MLA_HEADS // 2):
        a0, a1 = acc_s[2 * hp], acc_s[2 * hp + 1]
        o0 = a0 * (1.0 / a0[:, MLA_V:MLA_V + 1])
        o1 = a1 * (1.0 / a1[:, 0:1])
        o_ref[0, :, hp * HEAD_PAD:(hp + 1) * HEAD_PAD] = jnp.where(lane < MLA_V, o0, o1)


def _attn_seq(q, k, v, kpre, vpre, tile):
    B, L, W = q.shape
    TQ = min(tile, L)
    n_pre = 0 if kpre is None else kpre.shape[0]
    seq = pl.BlockSpec((1, L, W), lambda b, i: (b, 0, 0))
    in_specs = [pl.BlockSpec((1, TQ, W), lambda b, i: (b, i, 0)), seq, seq]
    args = [q, k, v]
    if n_pre:
        pad = ((0, HEAD_PAD - n_pre), (0, 0))
        kpre, vpre = jnp.pad(kpre, pad), jnp.pad(vpre, pad)
        in_specs += [_full(kpre.shape), _full(vpre.shape)]
        args += [kpre, vpre]
    return pl.pallas_call(
        functools.partial(_attn_seq_kernel, n_pre=n_pre, single_tile=(L == TQ)),
        grid=(B, L // TQ),
        in_specs=in_specs,
        out_specs=pl.BlockSpec((1, TQ, MLA_INNER), lambda b, i: (b, i, 0)),
        out_shape=jax.ShapeDtypeStruct((B, L, MLA_INNER), F32),
        scratch_shapes=[pltpu.VMEM((MLA_HEADS, TQ, HEAD_PAD), F32),
                        pltpu.VMEM((MLA_HEADS, TQ, HEAD_PAD), F32)],
        compiler_params=_cparams(("parallel", "arbitrary")),
    )(*args)


def _merge_kernel(x_ref, yssd_ref, ymla_ref, normw_ref, wz_ref, wg_ref, wgate_ref, ssdnw_ref,
                  wps_ref, wpm_ref, wo_ref, fnw_ref, o_ref, *, final):
    x = x_ref[...]
    h = _rms(x, normw_ref[...]).astype(BF16)
    z = _dot(h, wz_ref[...])
    y_ssd = _rms(yssd_ref[...] * _silu(z), ssdnw_ref[...]).astype(BF16)
    y_mla = (ymla_ref[...] * _silu(_dot(h, wg_ref[...]))).astype(BF16)
    gate = _sigmoid(_dot(h, wgate_ref[...]))
    D = x.shape[-1]
    u = gate[:, :D] * _dot(y_ssd, wps_ref[...]) + gate[:, D:] * _dot(y_mla, wpm_ref[...])
    out = x + _dot(u.astype(BF16), wo_ref[...])
    if final:
        out = _rms(out, fnw_ref[...])
    o_ref[...] = out


def _merge_rows(x, y_ssd, y_mla, lw, fnw, final, tile):
    R, D = x.shape
    T = min(tile, R)
    row = lambda w: pl.BlockSpec((T, w), lambda r: (r, 0))
    weights = [lw['norm_w'], lw['w_z'], lw['w_g'], lw['w_gate'], lw['ssd_norm_w'],
               lw['w_proj_ssd'], lw['w_proj_mla'], lw['w_out'], fnw]
    return pl.pallas_call(
        functools.partial(_merge_kernel, final=final),
        grid=(R // T,),
        in_specs=[row(D), row(SSD_INNER), row(MLA_INNER)] + [_full(w.shape) for w in weights],
        out_specs=row(D),
        out_shape=jax.ShapeDtypeStruct((R, D), F32),
        compiler_params=_cparams(("parallel",)),
    )(x, y_ssd, y_mla, *weights)


def _sample_inproj_kernel(x_ref, normw_ref, wxbc_ref, convw_ref, convb_ref, ctx_ref,
                          wdt_ref, dtb_ref, wcq_ref, qnw_ref, wuqn_ref, wuqr_ref, wukT_ref,
                          wckv_ref, kvnw_ref, wkr_ref, rc_ref, rs1_ref, rs2_ref,
                          xbc_out, dt_out, cs_out, qlat_out, qr_out, ckv_out, kr_out):
    C = SSD_CONV_DIM
    h = _rms(x_ref[...], normw_ref[...]).astype(BF16)
    raw = _dot(h, wxbc_ref[...])
    acc = convb_ref[...] + raw * convw_ref[SSD_CONV - 1:SSD_CONV, :]
    for k in range(SSD_CONV - 1):
        acc = acc + ctx_ref[k] * convw_ref[k:k + 1, :]
    xbc_out[...] = _silu(acc)
    for k in range(SSD_CONV - 2):
        cs_out[k] = ctx_ref[k + 1]
    cs_out[SSD_CONV - 2] = raw
    dt_out[...] = _softplus(_dot(h, wdt_ref[...]) + dtb_ref[...])

    rc, rs1, rs2 = rc_ref[...], rs1_ref[...], rs2_ref[...]
    cq = _rms(_dot(h, wcq_ref[...]), qnw_ref[...]).astype(BF16)
    qscale = MLA_SCALE * LOG2E
    qn = (_dot(cq, wuqn_ref[...]) * qscale).astype(BF16)
    qr = _rope_lanes(_dot(cq, wuqr_ref[...]), jnp.tile(rc, (1, MLA_HEADS)),
                     jnp.tile(rs1, (1, MLA_HEADS)), jnp.tile(rs2, (1, MLA_HEADS))) * qscale
    for hh in range(MLA_HEADS):
        qlat_out[:, hh, :] = _dot(qn[:, hh * MLA_NOPE:(hh + 1) * MLA_NOPE], wukT_ref[hh])
        qr_out[:, hh, :] = qr[:, hh * LANES + ROPE_LANE0:hh * LANES + ROPE_LANE0 + MLA_ROPE]
    kr = _rope_lanes(_dot(h, wkr_ref[...]), rc, rs1, rs2)
    kr_out[...] = kr[:, ROPE_LANE0:ROPE_LANE0 + MLA_ROPE]
    ckv_out[...] = _rms(_dot(h, wckv_ref[...]), kvnw_ref[...])


def _sample_inproj(x, ctx, rope_tabs, lw):
    R, D = x.shape
    rc, rs1, rs2 = rope_tabs
    args = [x, lw['norm_w'], lw['w_xbc'], lw['conv_w'], lw['conv_b'], ctx, lw['w_dt'], lw['dt_bias'],
            lw['w_cq'], lw['q_norm_w'], lw['w_uq_nope'], lw['w_uq_rope_p'], lw['w_ukT'],
            lw['w_ckv'], lw['kv_norm_w'], lw['w_kr_p'], rc, rs1, rs2]
    out_shape = (jax.ShapeDtypeStruct((R, SSD_CONV_DIM), F32),
                 jax.ShapeDtypeStruct((R, SSD_HEADS), F32),
                 jax.ShapeDtypeStruct((SSD_CONV - 1, R, SSD_CONV_DIM), F32),
                 jax.ShapeDtypeStruct((R, MLA_HEADS, MLA_KV_LORA), F32),
                 jax.ShapeDtypeStruct((R, MLA_HEADS, MLA_ROPE), F32),
                 jax.ShapeDtypeStruct((R, MLA_KV_LORA), F32),
                 jax.ShapeDtypeStruct((R, MLA_ROPE), F32))
    return pl.pallas_call(
        _sample_inproj_kernel,
        grid=(1,),
        in_specs=[_full(a.shape) for a in args],
        out_specs=tuple(_full(s.shape) for s in out_shape),
        out_shape=out_shape,
        compiler_params=_cparams(("arbitrary",)),
    )(*args)


def _ssd_step_kernel(xbc_ref, dt_ref, alog_ref, dskip_ref, s_ref, y_out, s_out):
    NB = xbc_ref.shape[0]
    GW = HEADS_PER_GROUP * SSD_HEAD_DIM
    expand = _head_expand()
    dt_e = _dot_sel(dt_ref[...], expand)
    a_e = _dot_sel(-jnp.exp(alog_ref[...]), expand)
    xs = xbc_ref[:, 0:SSD_INNER]
    xdt = xs * dt_e
    dec = jnp.exp(dt_e * a_e)
    dxs = xs * dskip_ref[...]
    KR = 16
    row = lax.broadcasted_iota(jnp.int32, (KR, SSD_INNER), 0)
    col = lax.broadcasted_iota(jnp.int32, (KR, SSD_INNER), 1)
    gmask = jnp.where((row < 2 * SSD_GROUPS) & (col // GW == row % SSD_GROUPS), 1.0, 0.0)
    rown = lax.broadcasted_iota(jnp.int32, (KR, SSD_STATE), 0)
    dec_rows = jnp.where((rown >= 2 * SSD_GROUPS) & (rown < 2 * SSD_GROUPS + 3), 1.0, 0.0)
    for b in range(NB):
        xb = xdt[b:b + 1, :]
        x1 = xb.astype(BF16).astype(F32)
        d1, d2, d3 = [d.astype(F32) for d in _split3(dec[b:b + 1, :])]
        lhs = jnp.where(row < SSD_GROUPS, x1, xb - x1) * gmask
        lhs = lhs + jnp.where(row == 2 * SSD_GROUPS, d1,
                              jnp.where(row == 2 * SSD_GROUPS + 1, d2,
                                        jnp.where(row == 2 * SSD_GROUPS + 2, d3, 0.0)))
        bmat = jnp.zeros((KR, SSD_STATE), F32)
        cmat = jnp.zeros((KR, SSD_STATE), F32)
        for g in range(SSD_GROUPS):
            bg = xbc_ref[b:b + 1, SSD_INNER + g * SSD_STATE:SSD_INNER + (g + 1) * SSD_STATE]
            cg = xbc_ref[b:b + 1, SSD_INNER + SSD_BC + g * SSD_STATE:SSD_INNER + SSD_BC + (g + 1) * SSD_STATE]
            bmat = bmat + jnp.where((rown < 2 * SSD_GROUPS) & (rown % SSD_GROUPS == g), bg, 0.0)
            cmat = cmat + jnp.where(rown == g, cg, 0.0)
        rhs = jnp.concatenate([bmat, dec_rows], axis=1).astype(BF16)
        r = _dot_tn(lhs.astype(BF16), rhs)
        s_new = s_ref[b] * r[:, SSD_STATE:] + r[:, :SSD_STATE]
        s_out[b] = s_new
        yt = _dot_nt(cmat.astype(BF16), s_new.astype(BF16))
        y_out[b:b + 1, :] = jnp.sum(yt * gmask, axis=0, keepdims=True) + dxs[b:b + 1, :]


def _ssd_step(xbc_c, dt, state, lw, nb):
    R = xbc_c.shape[0]
    return pl.pallas_call(
        _ssd_step_kernel,
        grid=(R // nb,),
        in_specs=[pl.BlockSpec((nb, SSD_CONV_DIM), lambda i: (i, 0)),
                  pl.BlockSpec((nb, SSD_HEADS), lambda i: (i, 0)),
                  _full((1, SSD_HEADS)), _full((1, SSD_INNER)),
                  pl.BlockSpec((nb, SSD_INNER, SSD_STATE), lambda i: (i, 0, 0))],
        out_specs=(pl.BlockSpec((nb, SSD_INNER), lambda i: (i, 0)),
                   pl.BlockSpec((nb, SSD_INNER, SSD_STATE), lambda i: (i, 0, 0))),
        out_shape=(jax.ShapeDtypeStruct((R, SSD_INNER), F32),
                   jax.ShapeDtypeStruct((R, SSD_INNER, SSD_STATE), F32)),
        compiler_params=_cparams(("parallel",)),
    )(xbc_c, dt, lw['a_log'], lw['d_skip_e'], state)


def _decode_kernel(pt_ref, qlat_ref, qr_ref, ckvn_ref, krn_ref, cckv_hbm, ckr_hbm, o_ref,
                   ckv_buf, kr_buf, sems, m_s, l_s, acc_s, *, layer, G):
    b = pl.program_id(0)
    c = pl.program_id(1)
    NB = pl.num_programs(0)
    NC = pl.num_programs(1)
    step = b * NC + c
    n_slots = ckv_buf.shape[0]
    ahead = n_slots - 1
    slot = lax.rem(step, n_slots)

    P = cckv_hbm.shape[2]

    def start_page(chunk, g):
        sl = lax.rem(chunk, n_slots)
        page = pt_ref[lax.div(chunk, NC), lax.rem(chunk, NC) * G + g]
        r = pl.multiple_of(g * P, P)
        pltpu.make_async_copy(cckv_hbm.at[layer, page], ckv_buf.at[sl, pl.ds(r, P)], sems.at[sl, 0]).start()
        pltpu.make_async_copy(ckr_hbm.at[layer, page], kr_buf.at[sl, :, pl.ds(r, P)], sems.at[sl, 1]).start()

    @pl.when(step == 0)
    def _():
        def prime(i, carry):
            @pl.when(lax.div(i, G) < NB * NC)
            def _():
                start_page(lax.div(i, G), lax.rem(i, G))
            return carry
        lax.fori_loop(0, ahead * G, prime, 0)

    pltpu.make_async_copy(ckv_buf.at[slot], ckv_buf.at[slot], sems.at[slot, 0]).wait()
    pltpu.make_async_copy(kr_buf.at[slot], kr_buf.at[slot], sems.at[slot, 1]).wait()

    @pl.when(step + ahead < NB * NC)
    def _():
        for g in range(G):
            start_page(step + ahead, g)

    @pl.when(c == 0)
    def _():
        m_s[...] = jnp.full(m_s.shape, -jnp.inf, F32)
        l_s[...] = jnp.zeros(l_s.shape, F32)
        acc_s[...] = jnp.zeros(acc_s.shape, F32)

    ql = qlat_ref[0].astype(BF16)
    qr = qr_ref[0].astype(BF16)
    kc = ckv_buf[slot].astype(BF16)
    s = _dot_nt(ql, kc) + _dot(qr, kr_buf[slot].astype(BF16))
    m, l, acc = m_s[...], l_s[...], acc_s[...]
    n_sub = 2 if G % 2 == 0 else 1
    W = G * P // n_sub
    for u in range(n_sub):
        su = s[:, u * W:(u + 1) * W]
        m_new = jnp.maximum(m, jnp.max(su, -1, keepdims=True))
        corr = jnp.exp2(m - m_new)
        p = jnp.exp2(su - m_new)
        l = l * corr + jnp.sum(p, -1, keepdims=True)
        acc = acc * corr + _dot(p.astype(BF16), kc[u * W:(u + 1) * W])
        m = m_new
    m_s[...], l_s[...], acc_s[...] = m, l, acc

    @pl.when(c == NC - 1)
    def _():
        kn = ckvn_ref[0]
        s1 = (jnp.sum(qlat_ref[0] * kn, -1, keepdims=True)
              + jnp.sum(qr_ref[0] * krn_ref[0], -1, keepdims=True))
        m2 = jnp.maximum(m, s1)
        corr2 = jnp.exp2(m - m2)
        p1 = jnp.exp2(s1 - m2)
        o_ref[0] = (acc * corr2 + p1 * kn) * (1.0 / (l * corr2 + p1))


def _decode_attn(page_table, qlat, qr, ckv_new, kr_new, cache_ckv, cache_kr, layer, G):
    R, n_pages = page_table.shape
    page = cache_ckv.shape[2]
    G = min(G, n_pages)
    grid_spec = pltpu.PrefetchScalarGridSpec(
        num_scalar_prefetch=1,
        grid=(R, n_pages // G),
        in_specs=[pl.BlockSpec((1, MLA_HEADS, MLA_KV_LORA), lambda b, c, pt: (b, 0, 0)),
                  pl.BlockSpec((1, MLA_HEADS, MLA_ROPE), lambda b, c, pt: (b, 0, 0)),
                  pl.BlockSpec((1, 1, MLA_KV_LORA), lambda b, c, pt: (b, 0, 0)),
                  pl.BlockSpec((1, 1, MLA_ROPE), lambda b, c, pt: (b, 0, 0)),
                  pl.BlockSpec(memory_space=pl.ANY),
                  pl.BlockSpec(memory_space=pl.ANY)],
        out_specs=pl.BlockSpec((1, MLA_HEADS, MLA_KV_LORA), lambda b, c, pt: (b, 0, 0)),
        scratch_shapes=[pltpu.VMEM((DECODE_SLOTS, G * page, MLA_KV_LORA), F32),
                        pltpu.VMEM((DECODE_SLOTS, MLA_ROPE, G * page), F32),
                        pltpu.SemaphoreType.DMA((DECODE_SLOTS, 2)),
                        pltpu.VMEM((MLA_HEADS, 1), F32),
                        pltpu.VMEM((MLA_HEADS, 1), F32),
                        pltpu.VMEM((MLA_HEADS, MLA_KV_LORA), F32)])
    return pl.pallas_call(
        functools.partial(_decode_kernel, layer=layer, G=G),
        grid_spec=grid_spec,
        out_shape=jax.ShapeDtypeStruct((R, MLA_HEADS, MLA_KV_LORA), F32),
        compiler_params=_cparams(("arbitrary", "arbitrary")),
    )(page_table, qlat, qr, ckv_new.reshape(R, 1, MLA_KV_LORA), kr_new.reshape(R, 1, MLA_ROPE),
      cache_ckv, jnp.swapaxes(cache_kr, 2, 3))


def _uv_proj_kernel(o_ref, wuv_ref, y_ref):
    for hh in range(MLA_HEADS):
        y_ref[:, hh * MLA_V:(hh + 1) * MLA_V] = _dot(o_ref[:, hh, :].astype(BF16), wuv_ref[hh])


def _uv_proj(o_lat, w_uv_h):
    R = o_lat.shape[0]
    return pl.pallas_call(
        _uv_proj_kernel,
        grid=(1,),
        in_specs=[_full(o_lat.shape), _full(w_uv_h.shape)],
        out_specs=_full((R, MLA_INNER)),
        out_shape=jax.ShapeDtypeStruct((R, MLA_INNER), F32),
        compiler_params=_cparams(("arbitrary",)),
    )(o_lat, w_uv_h)


def _rope_tables(pos, lane0):
    inv = 1.0 / (ROPE_BASE ** (jnp.arange(ROPE_HALF, dtype=F32) / ROPE_HALF))
    ang = pos.astype(F32)[:, None] * inv[None, :]
    cos, sin = jnp.cos(ang), jnp.sin(ang)
    n = pos.shape[0]
    z = lambda w: jnp.zeros((n, w), F32)
    pre = jnp.ones((n, lane0), F32)
    tail = LANES - lane0 - MLA_ROPE
    rc = jnp.concatenate([pre, cos, cos, z(tail)], 1)
    rs1 = jnp.concatenate([z(lane0), -sin, z(ROPE_HALF), z(tail)], 1)
    rs2 = jnp.concatenate([z(lane0), z(ROPE_HALF), sin, z(tail)], 1)
    return rc, rs1, rs2


def _layer_weights(p, l):
    D = p['w_in'].shape[1]
    w_in = p['w_in'][l]
    offs = [0]
    for n in (SSD_INNER, SSD_CONV_DIM, SSD_HEADS, MLA_Q_LORA, MLA_KV_LORA, MLA_ROPE, MLA_INNER, 2 * D):
        offs.append(offs[-1] + n)
    w_z, w_xbc, w_dt, w_cq, w_ckv, w_kr, w_g, w_gate = [w_in[:, offs[i]:offs[i + 1]] for i in range(8)]
    bf = lambda a: a.astype(BF16)
    pad_head = lambda a: jnp.pad(a, ((0, 0), (0, 0), (0, HEAD_PAD - a.shape[-1])))
    w_uq = p['w_uq'][l].reshape(MLA_Q_LORA, MLA_HEADS, MLA_NOPE + MLA_ROPE)
    w_uk = p['w_uk'][l].reshape(MLA_KV_LORA, MLA_HEADS, MLA_NOPE)
    w_uv = p['w_uv'][l]
    w_uv_pairs = w_uv.reshape(MLA_KV_LORA, MLA_HEADS // 2, 2, MLA_V)
    zv = jnp.zeros((MLA_KV_LORA, MLA_HEADS // 2, MLA_V), F32)
    w_uv_p = jnp.stack([jnp.concatenate([w_uv_pairs[:, :, 0], zv], -1),
                        jnp.concatenate([zv, w_uv_pairs[:, :, 1]], -1)], 2)
    one_at = lambda lane: (jnp.arange(HEAD_PAD) == lane).astype(F32)
    v_ones = jnp.tile(jnp.concatenate([one_at(MLA_V), one_at(0)]), MLA_HEADS // 2)[None]
    w_kr_p = jnp.pad(w_kr, ((0, 0), (ROPE_LANE0, LANES - ROPE_LANE0 - MLA_ROPE)))
    w_uq_rope_p = jnp.pad(w_uq[:, :, MLA_NOPE:], ((0, 0), (0, 0), (ROPE_LANE0, LANES - ROPE_LANE0 - MLA_ROPE)))
    return {
        'norm_w': p['norm_w'][l][None], 'w_xbc': bf(w_xbc), 'conv_w': p['conv_w'][l],
        'conv_b': p['conv_b'][l][None], 'w_dt': bf(w_dt), 'dt_bias': p['dt_bias'][l][None],
        'w_dtT': bf(w_dt.T), 'dt_biasT': p['dt_bias'][l][:, None],
        'w_cq': bf(w_cq), 'q_norm_w': p['q_norm_w'][l][None],
        'w_uq_p': bf(pad_head(w_uq).reshape(MLA_Q_LORA, MLA_HEADS * HEAD_PAD)),
        'w_uq_nope': bf(w_uq[:, :, :MLA_NOPE].reshape(MLA_Q_LORA, MLA_HEADS * MLA_NOPE)),
        'w_uq_rope_p': bf(w_uq_rope_p.reshape(MLA_Q_LORA, MLA_HEADS * LANES)),
        'w_ckv': bf(w_ckv), 'kv_norm_w': p['kv_norm_w'][l][None],
        'w_uk_p': bf(pad_head(w_uk).reshape(MLA_KV_LORA, MLA_HEADS * HEAD_PAD)),
        'w_ukT': bf(jnp.transpose(w_uk, (1, 2, 0))),
        'w_uv_p': bf(w_uv_p.reshape(MLA_KV_LORA, MLA_HEADS * HEAD_PAD)), 'v_ones': v_ones,
        'w_uv_h': bf(jnp.transpose(w_uv.reshape(MLA_KV_LORA, MLA_HEADS, MLA_V), (1, 0, 2))),
        'w_kr_p': bf(w_kr_p),
        'a_log': p['a_log'][l][None], 'a_logT': p['a_log'][l][:, None],
        'd_skip_e': jnp.repeat(p['d_skip'][l], SSD_HEAD_DIM)[None],
        'w_z': bf(w_z), 'w_g': bf(w_g), 'w_gate': bf(w_gate),
        'ssd_norm_w': p['ssd_norm_w'][l][None],
        'w_proj_ssd': bf(p['w_proj_ssd'][l]), 'w_proj_mla': bf(p['w_proj_mla'][l]), 'w_out': bf(p['w_out'][l]),
    }


SEQ_TILE = 512
ATTN_TILE = 256
MERGE_TILE = 512
STEP_ROWS = 8
SSD_CHUNKS_PER_STEP = 2
PAGES_PER_STEP = 32
DECODE_SLOTS = 3


def _seq_layer(x, ctx0, s0T, kpre, vpre, rope_tabs, lw, fnw, final, chunk, need_out=True):
    B, L, D = x.shape
    xbc_c, dt, dtT, q, k, v, ckv, kr, conv_state = _seq_inproj(x, ctx0, rope_tabs, lw, SEQ_TILE)
    y_ssd, sT = _ssd_seq(xbc_c, dt, dtT, s0T, lw, chunk)
    if not need_out:
        return None, (ckv, kr, sT, conv_state, k, v)
    y_mla = _attn_seq(q, k, v, kpre, vpre, ATTN_TILE)
    out = _merge_rows(x.reshape(B * L, D), y_ssd.reshape(B * L, SSD_INNER), y_mla.reshape(B * L, MLA_INNER),
                      lw, fnw, final, MERGE_TILE).reshape(B, L, D)
    return out, (ckv, kr, sT, conv_state, k, v)


def kernel(x_prompt, x_sample, cache_ckv, cache_kr, state_ssm, state_conv, page_table, meta_tokens,
           norm_w, w_in, conv_w, conv_b, dt_bias, a_log, d_skip, ssd_norm_w, q_norm_w, w_uq, kv_norm_w,
           w_uk, w_uv, w_proj_ssd, w_proj_mla, w_out, final_norm_w):
    p = {'norm_w': norm_w, 'w_in': w_in, 'conv_w': conv_w, 'conv_b': conv_b, 'dt_bias': dt_bias,
         'a_log': a_log, 'd_skip': d_skip, 'ssd_norm_w': ssd_norm_w, 'q_norm_w': q_norm_w, 'w_uq': w_uq,
         'kv_norm_w': kv_norm_w, 'w_uk': w_uk, 'w_uv': w_uv, 'w_proj_ssd': w_proj_ssd,
         'w_proj_mla': w_proj_mla, 'w_out': w_out}
    depth = w_in.shape[0]
    B, L, D = x_prompt.shape
    R = x_sample.shape[0]
    n_meta = meta_tokens.shape[0]
    past_len = page_table.shape[1] * cache_ckv.shape[2]
    fnw = final_norm_w[None]

    tabs_meta = _rope_tables(jnp.arange(n_meta), ROPE_LANE0)
    tabs_prompt = _rope_tables(n_meta + jnp.arange(L), ROPE_LANE0)
    tabs_sample = _rope_tables(past_len + jnp.arange(1), ROPE_LANE0)

    hm = meta_tokens[None].astype(x_prompt.dtype)
    hp = x_prompt
    hs = x_sample.reshape(R, D)
    outs = {k: [] for k in ('ckv_p', 'kr_p', 'ssm_p', 'conv_p', 'ckv_s', 'kr_s', 'ssm_s', 'conv_s')}
    for l in range(depth):
        lw = _layer_weights(p, l)
        final = l == depth - 1
        zero_ctx = jnp.zeros((SSD_CONV - 1, SSD_CONV_DIM), F32)
        zero_state = jnp.zeros((SSD_STATE, SSD_INNER), F32)
        hm_next, (ckv_m, kr_m, s_m, cs_m, k_m, v_m) = _seq_layer(
            hm, zero_ctx, zero_state, None, None, tabs_meta, lw, fnw, False, n_meta, need_out=not final)
        hp, (ckv_pp, kr_pp, s_p, cs_p, _, _) = _seq_layer(
            hp, cs_m[0], s_m[0].T, k_m[0], v_m[0], tabs_prompt, lw, fnw, final, SSD_CHUNK)
        hm = hm_next
        outs['ckv_p'].append(jnp.concatenate([jnp.broadcast_to(ckv_m, (B,) + ckv_m.shape[1:]), ckv_pp], 1))
        outs['kr_p'].append(jnp.concatenate([jnp.broadcast_to(kr_m, (B,) + kr_m.shape[1:]), kr_pp], 1))
        outs['ssm_p'].append(s_p.reshape(B, SSD_HEADS, SSD_HEAD_DIM, SSD_STATE))
        outs['conv_p'].append(cs_p)

        ctx = jnp.swapaxes(state_conv[l], 0, 1)
        xbc_c, dt, cs_s, qlat, qr, ckv_s, kr_s = _sample_inproj(hs, ctx, tabs_sample, lw)
        y_ssd, s_new = _ssd_step(xbc_c, dt, state_ssm[l].reshape(R, SSD_INNER, SSD_STATE), lw, STEP_ROWS)
        o_lat = _decode_attn(page_table, qlat, qr, ckv_s, kr_s, cache_ckv, cache_kr, l, PAGES_PER_STEP)
        y_mla = _uv_proj(o_lat, lw['w_uv_h'])
        hs = _merge_rows(hs, y_ssd, y_mla, lw, fnw, final, MERGE_TILE)
        outs['ckv_s'].append(ckv_s.reshape(R, 1, MLA_KV_LORA))
        outs['kr_s'].append(kr_s.reshape(R, 1, MLA_ROPE))
        outs['ssm_s'].append(s_new.reshape(R, SSD_HEADS, SSD_HEAD_DIM, SSD_STATE))
        outs['conv_s'].append(jnp.swapaxes(cs_s, 0, 1))

    st = lambda k: jnp.stack(outs[k])
    return (hp, hs.reshape(R, 1, D), st('ckv_p'), st('kr_p'), st('ssm_p'), st('conv_p'),
            st('ckv_s'), st('kr_s'), st('ssm_s'), st('conv_s'))
```

```python
import functools
import math

import jax
import jax.numpy as jnp
from jax import lax
from jax.experimental import pallas as pl
from jax.experimental.pallas import tpu as pltpu

N_META = 16
NORM_EPS = 1e-6
SSD_HEADS = 16
SSD_HEAD_DIM = 64
SSD_INNER = SSD_HEADS * SSD_HEAD_DIM
SSD_GROUPS = 4
HEADS_PER_GROUP = SSD_HEADS // SSD_GROUPS
SSD_STATE = 128
SSD_CONV = 4
SSD_CHUNK = 128
SSD_BC = SSD_GROUPS * SSD_STATE
SSD_CONV_DIM = SSD_INNER + 2 * SSD_BC
MLA_HEADS = 8
MLA_NOPE = 64
MLA_ROPE = 32
MLA_V = 64
MLA_Q_LORA = 384
MLA_KV_LORA = 256
MLA_INNER = MLA_HEADS * MLA_V
MLA_SCALE = (MLA_NOPE + MLA_ROPE) ** -0.5
ROPE_BASE = 10000.0
LOG2E = math.log2(math.e)

LANES = 128
HEAD_PAD = LANES
ROPE_LANE0 = MLA_NOPE
ROPE_HALF = MLA_ROPE // 2
CTX_ROW0 = 8 - (SSD_CONV - 1)
VMEM_LIMIT = 56 * 1024 * 1024

BF16 = jnp.bfloat16
F32 = jnp.float32


def _cparams(sem):
    return pltpu.CompilerParams(dimension_semantics=sem, vmem_limit_bytes=VMEM_LIMIT)


def _full(shape):
    n = len(shape)
    return pl.BlockSpec(shape, lambda *_: (0,) * n)


def _rms(x, w):
    return x * lax.rsqrt(jnp.mean(x * x, -1, keepdims=True) + NORM_EPS) * w


def _sigmoid(x):
    return 0.5 + 0.5 * jnp.tanh(0.5 * x)


def _silu(x):
    return x * _sigmoid(x)


def _softplus(x):
    return jnp.maximum(x, 0.0) + jnp.log(1.0 + jnp.exp(-jnp.abs(x)))


def _dot(a, b):
    return jnp.dot(a, b, preferred_element_type=F32)


def _dot_nt(a, b):
    return lax.dot_general(a, b, (((1,), (1,)), ((), ())), preferred_element_type=F32)


def _dot_tn(a, b):
    return lax.dot_general(a, b, (((0,), (0,)), ((), ())), preferred_element_type=F32)


def _heads(x):
    return jnp.stack([x[:, hh * HEAD_PAD:(hh + 1) * HEAD_PAD] for hh in range(MLA_HEADS)])


def _bdot_nt(a, b):
    return lax.dot_general(a, b, (((2,), (2,)), ((0,), (0,))), preferred_element_type=F32)


def _bdot(a, b):
    return lax.dot_general(a, b, (((2,), (1,)), ((0,), (0,))), preferred_element_type=F32)


def _split3(a):
    a1 = a.astype(BF16)
    r = a - a1.astype(F32)
    a2 = r.astype(BF16)
    a3 = (r - a2.astype(F32)).astype(BF16)
    return a1, a2, a3


def _dot_sel(a, sel):
    a1, a2, a3 = _split3(a)
    return _dot(a1, sel) + _dot(a2, sel) + _dot(a3, sel)


def _dot_sel2(a, sel):
    a1 = a.astype(BF16)
    a2 = (a - a1.astype(F32)).astype(BF16)
    return _dot(a1, sel) + _dot(a2, sel)


def _sel_dot(sel, a):
    a1, a2, a3 = _split3(a)
    return _dot(sel, a1) + _dot(sel, a2) + _dot(sel, a3)


def _head_expand():
    r = lax.broadcasted_iota(jnp.int32, (SSD_HEADS, SSD_INNER), 0)
    c = lax.broadcasted_iota(jnp.int32, (SSD_HEADS, SSD_INNER), 1)
    return jnp.where(c // SSD_HEAD_DIM == r, 1.0, 0.0).astype(BF16)


def _rope_lanes(x, c, s1, s2):
    n = x.shape[-1]
    return x * c + pltpu.roll(x, n - ROPE_HALF, 1) * s1 + pltpu.roll(x, ROPE_HALF, 1) * s2


def _seq_inproj_kernel(x_ref, normw_ref, wxbc_ref, convw_ref, convb_ref, ctx0_ref,
                       wdt_ref, dtb_ref, wdtT_ref, dtbT_ref,
                       wcq_ref, qnw_ref, wuq_ref, wckv_ref, kvnw_ref, wuk_ref, wuv_ref, vone_ref, wkr_ref,
                       rc_ref, rs1_ref, rs2_ref,
                       xbc_out, dt_out, dtT_out, q_out, k_out, v_out, ckv_out, kr_out, cs_out,
                       cbuf):
    t = pl.program_id(1)
    T = x_ref.shape[1]
    nctx = SSD_CONV - 1
    h = _rms(x_ref[0], normw_ref[...]).astype(BF16)

    @pl.when(t == 0)
    def _():
        cbuf[CTX_ROW0:8, :] = ctx0_ref[...]

    cbuf[8:8 + T, :] = _dot(h, wxbc_ref[...])
    acc = convb_ref[...]
    for k in range(SSD_CONV):
        acc = acc + cbuf[CTX_ROW0 + k:CTX_ROW0 + k + T, :] * convw_ref[k:k + 1, :]
    xbc_out[0] = acc
    last = cbuf[8 + T - nctx:8 + T, :]
    cs_out[0] = last
    cbuf[CTX_ROW0:8, :] = last

    dt_out[0] = _softplus(_dot(h, wdt_ref[...]) + dtb_ref[...])
    dtT_out[0] = _softplus(_dot_nt(wdtT_ref[...], h) + dtbT_ref[...])

    rc, rs1, rs2 = rc_ref[...], rs1_ref[...], rs2_ref[...]
    cq = _rms(_dot(h, wcq_ref[...]), qnw_ref[...]).astype(BF16)
    q = _dot(cq, wuq_ref[...])
    q = _rope_lanes(q, jnp.tile(rc, (1, MLA_HEADS)), jnp.tile(rs1, (1, MLA_HEADS)),
                    jnp.tile(rs2, (1, MLA_HEADS)))
    q_out[0] = (q * (MLA_SCALE * LOG2E)).astype(BF16)

    kr = _rope_lanes(_dot(h, wkr_ref[...]), rc, rs1, rs2)
    kr_out[0] = kr[:, ROPE_LANE0:ROPE_LANE0 + MLA_ROPE]
    ckv = _rms(_dot(h, wckv_ref[...]), kvnw_ref[...])
    ckv_out[0] = ckv
    cb = ckv.astype(BF16)
    k_out[0] = (_dot(cb, wuk_ref[...]) + jnp.tile(kr, (1, MLA_HEADS))).astype(BF16)
    v_out[0] = (_dot(cb, wuv_ref[...]) + vone_ref[...]).astype(BF16)


def _seq_inproj(x, ctx0, rope_tabs, lw, tile):
    B, L, D = x.shape
    T = min(tile, L)
    rc, rs1, rs2 = rope_tabs
    row = lambda w: pl.BlockSpec((1, T, w), lambda b, t: (b, t, 0))
    tab = pl.BlockSpec((T, LANES), lambda b, t: (t, 0))
    weights = [lw['norm_w'], lw['w_xbc'], lw['conv_w'], lw['conv_b'], ctx0,
               lw['w_dt'], lw['dt_bias'], lw['w_dtT'], lw['dt_biasT'],
               lw['w_cq'], lw['q_norm_w'], lw['w_uq_p'], lw['w_ckv'], lw['kv_norm_w'],
               lw['w_uk_p'], lw['w_uv_p'], lw['v_ones'], lw['w_kr_p']]
    out_shape = (
        jax.ShapeDtypeStruct((B, L, SSD_CONV_DIM), F32),
        jax.ShapeDtypeStruct((B, L, SSD_HEADS), F32),
        jax.ShapeDtypeStruct((B, SSD_HEADS, L), F32),
        jax.ShapeDtypeStruct((B, L, MLA_HEADS * HEAD_PAD), BF16),
        jax.ShapeDtypeStruct((B, L, MLA_HEADS * HEAD_PAD), BF16),
        jax.ShapeDtypeStruct((B, L, MLA_HEADS * HEAD_PAD), BF16),
        jax.ShapeDtypeStruct((B, L, MLA_KV_LORA), F32),
        jax.ShapeDtypeStruct((B, L, MLA_ROPE), F32),
        jax.ShapeDtypeStruct((B, SSD_CONV - 1, SSD_CONV_DIM), F32),
    )
    out_specs = (row(SSD_CONV_DIM), row(SSD_HEADS),
                 pl.BlockSpec((1, SSD_HEADS, T), lambda b, t: (b, 0, t)),
                 row(MLA_HEADS * HEAD_PAD), row(MLA_HEADS * HEAD_PAD), row(MLA_HEADS * HEAD_PAD),
                 row(MLA_KV_LORA), row(MLA_ROPE),
                 pl.BlockSpec((1, SSD_CONV - 1, SSD_CONV_DIM), lambda b, t: (b, 0, 0)))
    return pl.pallas_call(
        _seq_inproj_kernel,
        grid=(B, L // T),
        in_specs=[row(D)] + [_full(w.shape) for w in weights] + [tab, tab, tab],
        out_specs=out_specs,
        out_shape=out_shape,
        scratch_shapes=[pltpu.VMEM((8 + T, SSD_CONV_DIM), F32)],
        compiler_params=_cparams(("parallel", "arbitrary")),
    )(x, *weights, rc, rs1, rs2)


def _ssd_seq_kernel(xbc_ref, dtall_ref, dtTall_ref, dtT_ref, alog_ref, alogT_ref, dskip_ref, s0_ref,
                    y_out, s_out, sT, acs_s, acsT_s, dtw_s, cdec_s):
    c = pl.program_id(1)
    NCH, Q, _ = acs_s.shape
    CPS = xbc_ref.shape[1] // Q
    GW = HEADS_PER_GROUP * SSD_HEAD_DIM
    ri = lax.broadcasted_iota(jnp.int32, (Q, Q), 0)
    ci = lax.broadcasted_iota(jnp.int32, (Q, Q), 1)
    causal = ci <= ri

    @pl.when(c == 0)
    def _():
        sT[...] = s0_ref[...]
        tril = jnp.broadcast_to(jnp.where(causal, 1.0, 0.0).astype(BF16)[None], (NCH, Q, Q))
        triu = jnp.broadcast_to(jnp.where(ri <= ci, 1.0, 0.0).astype(BF16)[None], (NCH, Q, Q))
        dt3 = dtall_ref[0].reshape(NCH, Q, SSD_HEADS)
        a1, a2, a3 = _split3(dt3 * (-jnp.exp(alog_ref[...])))
        acs3 = _bdot(tril, a1) + _bdot(tril, a2) + _bdot(tril, a3)
        aT = dtTall_ref[0] * (-jnp.exp(alogT_ref[...]))
        b1, b2, b3 = _split3(jnp.stack([aT[:, k * Q:(k + 1) * Q] for k in range(NCH)]))
        acs_s[...] = acs3
        acsT_s[...] = _bdot(b1, triu) + _bdot(b2, triu) + _bdot(b3, triu)
        end3 = acs3[:, Q - 1:Q, :]
        dtw_s[...] = dt3 * jnp.exp(end3 - acs3)
        cdec_s[...] = jnp.exp(end3)

    expand = _head_expand()
    lane = lax.broadcasted_iota(jnp.int32, (Q, LANES), 1)
    for k in range(CPS):
        rows = slice(k * Q, (k + 1) * Q)
        ch = c * CPS + k
        acs = acs_s[ch]
        acsT = acsT_s[ch]
        dtT = dtT_ref[0, :, rows]
        dtw_e = _dot_sel2(dtw_s[ch], expand)
        chunk_decay = _dot_sel(cdec_s[ch], expand)
        xs = _silu(xbc_ref[0, rows, 0:SSD_INNER])
        xs_b = xs.astype(BF16)
        xw = (xs * dtw_e).astype(BF16)
        dxs = xs * dskip_ref[...]

        for g in range(SSD_GROUPS):
            lo = g * GW
            Bg = _silu(xbc_ref[0, rows, SSD_INNER + g * SSD_STATE:SSD_INNER + (g + 1) * SSD_STATE]).astype(BF16)
            Cg_f = _silu(xbc_ref[0, rows, SSD_INNER + SSD_BC + g * SSD_STATE:
                                 SSD_INNER + SSD_BC + (g + 1) * SSD_STATE])
            cb = _dot_nt(Cg_f.astype(BF16), Bg)
            heads = range(g * HEADS_PER_GROUP, (g + 1) * HEADS_PER_GROUP)
            acs_b = jnp.stack([jnp.broadcast_to(acs[:, hh:hh + 1], (Q, LANES)) for hh in heads])
            acs_r = jnp.stack([acsT[hh:hh + 1, :] for hh in heads])
            dt_r = jnp.stack([dtT[hh:hh + 1, :] for hh in heads])
            m = cb[None] * jnp.exp(jnp.where(causal[None], acs_b[:, :, :Q] - acs_r, -jnp.inf)) * dt_r
            lhs = jnp.concatenate([Cg_f[None] * jnp.exp(acs_b), m], axis=2).astype(BF16)
            pair_rhs = [jnp.concatenate([sT[:, lo + pr * LANES:lo + (pr + 1) * LANES].astype(BF16),
                                         xs_b[:, lo + pr * LANES:lo + (pr + 1) * LANES]], axis=0)
                        for pr in range(HEADS_PER_GROUP // 2)]
            y = _bdot(lhs, jnp.stack([pair_rhs[e // 2] for e in range(HEADS_PER_GROUP)]))
            for pr in range(HEADS_PER_GROUP // 2):
                plo = lo + pr * LANES
                y_out[0, rows, plo:plo + LANES] = (jnp.where(lane < SSD_HEAD_DIM, y[2 * pr], y[2 * pr + 1])
                                                   + dxs[:, plo:plo + LANES])
            sT[:, lo:lo + GW] = sT[:, lo:lo + GW] * chunk_decay[:, lo:lo + GW] + _dot_tn(Bg, xw[:, lo:lo + GW])

    @pl.when(c == pl.num_programs(1) - 1)
    def _():
        for t in range(SSD_INNER // LANES):
            s_out[0, t * LANES:(t + 1) * LANES, :] = sT[:, t * LANES:(t + 1) * LANES].T


def _ssd_seq(xbc_c, dt, dtT, s0T, lw, chunk):
    B, L, _ = xbc_c.shape
    Q = min(chunk, L)
    R = Q * min(SSD_CHUNKS_PER_STEP, L // Q)
    return pl.pallas_call(
        _ssd_seq_kernel,
        grid=(B, L // R),
        in_specs=[pl.BlockSpec((1, R, SSD_CONV_DIM), lambda b, c: (b, c, 0)),
                  pl.BlockSpec((1, L, SSD_HEADS), lambda b, c: (b, 0, 0)),
                  pl.BlockSpec((1, SSD_HEADS, L), lambda b, c: (b, 0, 0)),
                  pl.BlockSpec((1, SSD_HEADS, R), lambda b, c: (b, 0, c)),
                  _full((1, SSD_HEADS)), _full((SSD_HEADS, 1)), _full((1, SSD_INNER)),
                  _full((SSD_STATE, SSD_INNER))],
        out_specs=(pl.BlockSpec((1, R, SSD_INNER), lambda b, c: (b, c, 0)),
                   pl.BlockSpec((1, SSD_INNER, SSD_STATE), lambda b, c: (b, 0, 0))),
        out_shape=(jax.ShapeDtypeStruct((B, L, SSD_INNER), F32),
                   jax.ShapeDtypeStruct((B, SSD_INNER, SSD_STATE), F32)),
        scratch_shapes=[pltpu.VMEM((SSD_STATE, SSD_INNER), F32),
                        pltpu.VMEM((L // Q, Q, SSD_HEADS), F32),
                        pltpu.VMEM((L // Q, SSD_HEADS, Q), F32),
                        pltpu.VMEM((L // Q, Q, SSD_HEADS), F32),
                        pltpu.VMEM((L // Q, 1, SSD_HEADS), F32)],
        compiler_params=_cparams(("parallel", "arbitrary")),
    )(xbc_c, dt, dtT, dtT, lw['a_log'], lw['a_logT'], lw['d_skip_e'], s0T)


def _attn_seq_kernel(*refs, n_pre, single_tile):
    if n_pre:
        q_ref, k_ref, v_ref, kpre_ref, vpre_ref, o_ref, m_s, acc_s = refs
    else:
        q_ref, k_ref, v_ref, o_ref, m_s, acc_s = refs
    i = pl.program_id(1)
    TQ = q_ref.shape[1]
    TKB = 2 * TQ
    causal = (lax.broadcasted_iota(jnp.int32, (TQ, TQ), 1)
              <= lax.broadcasted_iota(jnp.int32, (TQ, TQ), 0))[None]
    mshape = (MLA_HEADS, TQ, HEAD_PAD)
    m_s[...] = jnp.full(mshape, -jnp.inf, F32)
    acc_s[...] = jnp.zeros(mshape, F32)

    def lanes_of(m, n):
        return m[..., :n] if n <= HEAD_PAD else jnp.concatenate([m] * (n // HEAD_PAD), -1)

    def update(s, vs):
        m_old = m_s[...]
        m_new = jnp.maximum(m_old, jnp.max(s, -1, keepdims=True))
        p = jnp.exp2(s - lanes_of(m_new, s.shape[-1])).astype(BF16)
        acc_s[...] = acc_s[...] * jnp.exp2(m_old - m_new) + _bdot(p, vs)
        m_s[...] = m_new

    def body(j, carry):
        r = pl.multiple_of(j * TKB, TKB)
        update(_bdot_nt(_heads(q_ref[0]), _heads(k_ref[0, pl.ds(r, TKB), :])),
               _heads(v_ref[0, pl.ds(r, TKB), :]))
        return carry

    lax.fori_loop(0, lax.div(i, 2), body, 0)

    def tail(with_full):
        q3 = _heads(q_ref[0])
        ss, vv = [], []
        if n_pre:
            col = lax.broadcasted_iota(jnp.int32, (TQ, kpre_ref.shape[0]), 1)
            ss.append(jnp.where((col < n_pre)[None], _bdot_nt(q3, _heads(kpre_ref[...])), -jnp.inf))
            vv.append(_heads(vpre_ref[...]))
        r_diag = pl.multiple_of(i * TQ, TQ)
        if with_full:
            r_full = pl.multiple_of(r_diag - TQ, TQ)
            ss.append(_bdot_nt(q3, _heads(k_ref[0, pl.ds(r_full, TQ), :])))
            vv.append(_heads(v_ref[0, pl.ds(r_full, TQ), :]))
        ss.append(jnp.where(causal, _bdot_nt(q3, _heads(k_ref[0, pl.ds(r_diag, TQ), :])), -jnp.inf))
        vv.append(_heads(v_ref[0, pl.ds(r_diag, TQ), :]))
        update(jnp.concatenate(ss, -1), jnp.concatenate(vv, 1))

    if single_tile:
        tail(False)
    else:
        pl.when(lax.rem(i, 2) == 0)(lambda: tail(False))
        pl.when(lax.rem(i, 2) == 1)(lambda: tail(True))

    lane = lax.broadcasted_iota(jnp.int32, (TQ, HEAD_PAD), 1)
    for hp in range(MLA_HEADS // 2):
        a0, a1 = acc_s[2 * hp], acc_s[2 * hp + 1]
        o0 = a0 * (1.0 / a0[:, MLA_V:MLA_V + 1])
        o1 = a1 * (1.0 / a1[:, 0:1])
        o_ref[0, :, hp * HEAD_PAD:(hp + 1) * HEAD_PAD] = jnp.where(lane < MLA_V, o0, o1)


def _attn_seq(q, k, v, kpre, vpre, tile):
    B, L, W = q.shape
    TQ = min(tile, L)
    n_pre = 0 if kpre is None else kpre.shape[0]
    seq = pl.BlockSpec((1, L, W), lambda b, i: (b, 0, 0))
    in_specs = [pl.BlockSpec((1, TQ, W), lambda b, i: (b, i, 0)), seq, seq]
    args = [q, k, v]
    if n_pre:
        pad = ((0, HEAD_PAD - n_pre), (0, 0))
        kpre, vpre = jnp.pad(kpre, pad), jnp.pad(vpre, pad)
        in_specs += [_full(kpre.shape), _full(vpre.shape)]
        args += [kpre, vpre]
    return pl.pallas_call(
        functools.partial(_attn_seq_kernel, n_pre=n_pre, single_tile=(L == TQ)),
        grid=(B, L // TQ),
        in_specs=in_specs,
        out_specs=pl.BlockSpec((1, TQ, MLA_INNER), lambda b, i: (b, i, 0)),
        out_shape=jax.ShapeDtypeStruct((B, L, MLA_INNER), F32),
        scratch_shapes=[pltpu.VMEM((MLA_HEADS, TQ, HEAD_PAD), F32),
                        pltpu.VMEM((MLA_HEADS, TQ, HEAD_PAD), F32)],
        compiler_params=_cparams(("parallel", "arbitrary")),
    )(*args)


def _merge_kernel(x_ref, yssd_ref, ymla_ref, normw_ref, wz_ref, wg_ref, wgate_ref, ssdnw_ref,
                  wps_ref, wpm_ref, wo_ref, fnw_ref, o_ref, *, final):
    x = x_ref[...]
    h = _rms(x, normw_ref[...]).astype(BF16)
    z = _dot(h, wz_ref[...])
    y_ssd = _rms(yssd_ref[...] * _silu(z), ssdnw_ref[...]).astype(BF16)
    y_mla = (ymla_ref[...] * _silu(_dot(h, wg_ref[...]))).astype(BF16)
    gate = _sigmoid(_dot(h, wgate_ref[...]))
    D = x.shape[-1]
    u = gate[:, :D] * _dot(y_ssd, wps_ref[...]) + gate[:, D:] * _dot(y_mla, wpm_ref[...])
    out = x + _dot(u.astype(BF16), wo_ref[...])
    if final:
        out = _rms(out, fnw_ref[...])
    o_ref[...] = out


def _merge_rows(x, y_ssd, y_mla, lw, fnw, final, tile):
    R, D = x.shape
    T = min(tile, R)
    row = lambda w: pl.BlockSpec((T, w), lambda r: (r, 0))
    weights = [lw['norm_w'], lw['w_z'], lw['w_g'], lw['w_gate'], lw['ssd_norm_w'],
               lw['w_proj_ssd'], lw['w_proj_mla'], lw['w_out'], fnw]
    return pl.pallas_call(
        functools.partial(_merge_kernel, final=final),
        grid=(R // T,),
        in_specs=[row(D), row(SSD_INNER), row(MLA_INNER)] + [_full(w.shape) for w in weights],
        out_specs=row(D),
        out_shape=jax.ShapeDtypeStruct((R, D), F32),
        compiler_params=_cparams(("parallel",)),
    )(x, y_ssd, y_mla, *weights)


def _sample_inproj_kernel(x_ref, normw_ref, wxbc_ref, convw_ref, convb_ref, ctx_ref,
                          wdt_ref, dtb_ref, wcq_ref, qnw_ref, wuqn_ref, wuqr_ref, wukT_ref,
                          wckv_ref, kvnw_ref, wkr_ref, rc_ref, rs1_ref, rs2_ref,
                          xbc_out, dt_out, cs_out, qlat_out, qr_out, ckv_out, kr_out):
    C = SSD_CONV_DIM
    h = _rms(x_ref[...], normw_ref[...]).astype(BF16)
    raw = _dot(h, wxbc_ref[...])
    acc = convb_ref[...] + raw * convw_ref[SSD_CONV - 1:SSD_CONV, :]
    for k in range(SSD_CONV - 1):
        acc = acc + ctx_ref[k] * convw_ref[k:k + 1, :]
    xbc_out[...] = _silu(acc)
    for k in range(SSD_CONV - 2):
        cs_out[k] = ctx_ref[k + 1]
    cs_out[SSD_CONV - 2] = raw
    dt_out[...] = _softplus(_dot(h, wdt_ref[...]) + dtb_ref[...])

    rc, rs1, rs2 = rc_ref[...], rs1_ref[...], rs2_ref[...]
    cq = _rms(_dot(h, wcq_ref[...]), qnw_ref[...]).astype(BF16)
    qscale = MLA_SCALE * LOG2E
    qn = (_dot(cq, wuqn_ref[...]) * qscale).astype(BF16)
    qr = _rope_lanes(_dot(cq, wuqr_ref[...]), jnp.tile(rc, (1, MLA_HEADS)),
                     jnp.tile(rs1, (1, MLA_HEADS)), jnp.tile(rs2, (1, MLA_HEADS))) * qscale
    for hh in range(MLA_HEADS):
        qlat_out[:, hh, :] = _dot(qn[:, hh * MLA_NOPE:(hh + 1) * MLA_NOPE], wukT_ref[hh])
        qr_out[:, hh, :] = qr[:, hh * LANES + ROPE_LANE0:hh * LANES + ROPE_LANE0 + MLA_ROPE]
    kr = _rope_lanes(_dot(h, wkr_ref[...]), rc, rs1, rs2)
    kr_out[...] = kr[:, ROPE_LANE0:ROPE_LANE0 + MLA_ROPE]
    ckv_out[...] = _rms(_dot(h, wckv_ref[...]), kvnw_ref[...])


def _sample_inproj(x, ctx, rope_tabs, lw):
    R, D = x.shape
    rc, rs1, rs2 = rope_tabs
    args = [x, lw['norm_w'], lw['w_xbc'], lw['conv_w'], lw['conv_b'], ctx, lw['w_dt'], lw['dt_bias'],
            lw['w_cq'], lw['q_norm_w'], lw['w_uq_nope'], lw['w_uq_rope_p'], lw['w_ukT'],
            lw['w_ckv'], lw['kv_norm_w'], lw['w_kr_p'], rc, rs1, rs2]
    out_shape = (jax.ShapeDtypeStruct((R, SSD_CONV_DIM), F32),
                 jax.ShapeDtypeStruct((R, SSD_HEADS), F32),
                 jax.ShapeDtypeStruct((SSD_CONV - 1, R, SSD_CONV_DIM), F32),
                 jax.ShapeDtypeStruct((R, MLA_HEADS, MLA_KV_LORA), F32),
                 jax.ShapeDtypeStruct((R, MLA_HEADS, MLA_ROPE), F32),
                 jax.ShapeDtypeStruct((R, MLA_KV_LORA), F32),
                 jax.ShapeDtypeStruct((R, MLA_ROPE), F32))
    return pl.pallas_call(
        _sample_inproj_kernel,
        grid=(1,),
        in_specs=[_full(a.shape) for a in args],
        out_specs=tuple(_full(s.shape) for s in out_shape),
        out_shape=out_shape,
        compiler_params=_cparams(("arbitrary",)),
    )(*args)


def _ssd_step_kernel(xbc_ref, dt_ref, alog_ref, dskip_ref, s_ref, y_out, s_out):
    NB = xbc_ref.shape[0]
    GW = HEADS_PER_GROUP * SSD_HEAD_DIM
    expand = _head_expand()
    dt_e = _dot_sel(dt_ref[...], expand)
    a_e = _dot_sel(-jnp.exp(alog_ref[...]), expand)
    xs = xbc_ref[:, 0:SSD_INNER]
    xdt = xs * dt_e
    dec = jnp.exp(dt_e * a_e)
    dxs = xs * dskip_ref[...]
    KR = 16
    row = lax.broadcasted_iota(jnp.int32, (KR, SSD_INNER), 0)
    col = lax.broadcasted_iota(jnp.int32, (KR, SSD_INNER), 1)
    gmask = jnp.where((row < 2 * SSD_GROUPS) & (col // GW == row % SSD_GROUPS), 1.0, 0.0)
    rown = lax.broadcasted_iota(jnp.int32, (KR, SSD_STATE), 0)
    dec_rows = jnp.where((rown >= 2 * SSD_GROUPS) & (rown < 2 * SSD_GROUPS + 3), 1.0, 0.0)
    for b in range(NB):
        xb = xdt[b:b + 1, :]
        x1 = xb.astype(BF16).astype(F32)
        d1, d2, d3 = [d.astype(F32) for d in _split3(dec[b:b + 1, :])]
        lhs = jnp.where(row < SSD_GROUPS, x1, xb - x1) * gmask
        lhs = lhs + jnp.where(row == 2 * SSD_GROUPS, d1,
                              jnp.where(row == 2 * SSD_GROUPS + 1, d2,
                                        jnp.where(row == 2 * SSD_GROUPS + 2, d3, 0.0)))
        bmat = jnp.zeros((KR, SSD_STATE), F32)
        cmat = jnp.zeros((KR, SSD_STATE), F32)
        for g in range(SSD_GROUPS):
            bg = xbc_ref[b:b + 1, SSD_INNER + g * SSD_STATE:SSD_INNER + (g + 1) * SSD_STATE]
            cg = xbc_ref[b:b + 1, SSD_INNER + SSD_BC + g * SSD_STATE:SSD_INNER + SSD_BC + (g + 1) * SSD_STATE]
            bmat = bmat + jnp.where((rown < 2 * SSD_GROUPS) & (rown % SSD_GROUPS == g), bg, 0.0)
            cmat = cmat + jnp.where(rown == g, cg, 0.0)
        rhs = jnp.concatenate([bmat, dec_rows], axis=1).astype(BF16)
        r = _dot_tn(lhs.astype(BF16), rhs)
        s_new = s_ref[b] * r[:, SSD_STATE:] + r[:, :SSD_STATE]
        s_out[b] = s_new
        yt = _dot_nt(cmat.astype(BF16), s_new.astype(BF16))
        y_out[b:b + 1, :] = jnp.sum(yt * gmask, axis=0, keepdims=True) + dxs[b:b + 1, :]


def _ssd_step(xbc_c, dt, state, layer, lw, nb):
    R = xbc_c.shape[0]
    return pl.pallas_call(
        _ssd_step_kernel,
        grid=(R // nb,),
        in_specs=[pl.BlockSpec((nb, SSD_CONV_DIM), lambda i: (i, 0)),
                  pl.BlockSpec((nb, SSD_HEADS), lambda i: (i, 0)),
                  _full((1, SSD_HEADS)), _full((1, SSD_INNER)),
                  pl.BlockSpec((None, nb, SSD_INNER, SSD_STATE), lambda i: (layer, i, 0, 0))],
        out_specs=(pl.BlockSpec((nb, SSD_INNER), lambda i: (i, 0)),
                   pl.BlockSpec((nb, SSD_INNER, SSD_STATE), lambda i: (i, 0, 0))),
        out_shape=(jax.ShapeDtypeStruct((R, SSD_INNER), F32),
                   jax.ShapeDtypeStruct((R, SSD_INNER, SSD_STATE), F32)),
        compiler_params=_cparams(("parallel",)),
    )(xbc_c, dt, lw['a_log'], lw['d_skip_e'], state)


def _decode_kernel(pt_ref, qlat_ref, qr_ref, ckvn_ref, krn_ref, cckv_hbm, ckr_hbm, o_ref,
                   ckv_buf, kr_buf, sems, m_s, l_s, acc_s, *, layer, G):
    b = pl.program_id(0)
    c = pl.program_id(1)
    NB = pl.num_programs(0)
    NC = pl.num_programs(1)
    step = b * NC + c
    n_slots = ckv_buf.shape[0]
    ahead = n_slots - 1
    slot = lax.rem(step, n_slots)

    P = cckv_hbm.shape[2]

    def start_page(chunk, g):
        sl = lax.rem(chunk, n_slots)
        page = pt_ref[lax.div(chunk, NC), lax.rem(chunk, NC) * G + g]
        r = pl.multiple_of(g * P, P)
        pltpu.make_async_copy(cckv_hbm.at[layer, page], ckv_buf.at[sl, pl.ds(r, P)], sems.at[sl, 0]).start()
        pltpu.make_async_copy(ckr_hbm.at[layer, page], kr_buf.at[sl, :, pl.ds(r, P)], sems.at[sl, 1]).start()

    @pl.when(step == 0)
    def _():
        def prime(i, carry):
            @pl.when(lax.div(i, G) < NB * NC)
            def _():
                start_page(lax.div(i, G), lax.rem(i, G))
            return carry
        lax.fori_loop(0, ahead * G, prime, 0)

    pltpu.make_async_copy(ckv_buf.at[slot], ckv_buf.at[slot], sems.at[slot, 0]).wait()
    pltpu.make_async_copy(kr_buf.at[slot], kr_buf.at[slot], sems.at[slot, 1]).wait()

    @pl.when(step + ahead < NB * NC)
    def _():
        for g in range(G):
            start_page(step + ahead, g)

    @pl.when(c == 0)
    def _():
        m_s[...] = jnp.full(m_s.shape, -jnp.inf, F32)
        l_s[...] = jnp.zeros(l_s.shape, F32)
        acc_s[...] = jnp.zeros(acc_s.shape, F32)

    ql = qlat_ref[0].astype(BF16)
    qr = qr_ref[0].astype(BF16)
    kc = ckv_buf[slot].astype(BF16)
    s = _dot_nt(ql, kc) + _dot(qr, kr_buf[slot].astype(BF16))
    m, l, acc = m_s[...], l_s[...], acc_s[...]
    n_sub = 2 if G % 2 == 0 else 1
    W = G * P // n_sub
    for u in range(n_sub):
        su = s[:, u * W:(u + 1) * W]
        m_new = jnp.maximum(m, jnp.max(su, -1, keepdims=True))
        corr = jnp.exp2(m - m_new)
        p = jnp.exp2(su - m_new)
        l = l * corr + jnp.sum(p, -1, keepdims=True)
        acc = acc * corr + _dot(p.astype(BF16), kc[u * W:(u + 1) * W])
        m = m_new
    m_s[...], l_s[...], acc_s[...] = m, l, acc

    @pl.when(c == NC - 1)
    def _():
        kn = ckvn_ref[0]
        s1 = (jnp.sum(qlat_ref[0] * kn, -1, keepdims=True)
              + jnp.sum(qr_ref[0] * krn_ref[0], -1, keepdims=True))
        m2 = jnp.maximum(m, s1)
        corr2 = jnp.exp2(m - m2)
        p1 = jnp.exp2(s1 - m2)
        o_ref[0] = (acc * corr2 + p1 * kn) * (1.0 / (l * corr2 + p1))


def _decode_attn(page_table, qlat, qr, ckv_new, kr_new, cache_ckv, cache_kr, layer, G):
    R, n_pages = page_table.shape
    page = cache_ckv.shape[2]
    G = min(G, n_pages)
    grid_spec = pltpu.PrefetchScalarGridSpec(
        num_scalar_prefetch=1,
        grid=(R, n_pages // G),
        in_specs=[pl.BlockSpec((1, MLA_HEADS, MLA_KV_LORA), lambda b, c, pt: (b, 0, 0)),
                  pl.BlockSpec((1, MLA_HEADS, MLA_ROPE), lambda b, c, pt: (b, 0, 0)),
                  pl.BlockSpec((1, 1, MLA_KV_LORA), lambda b, c, pt: (b, 0, 0)),
                  pl.BlockSpec((1, 1, MLA_ROPE), lambda b, c, pt: (b, 0, 0)),
                  pl.BlockSpec(memory_space=pl.ANY),
                  pl.BlockSpec(memory_space=pl.ANY)],
        out_specs=pl.BlockSpec((1, MLA_HEADS, MLA_KV_LORA), lambda b, c, pt: (b, 0, 0)),
        scratch_shapes=[pltpu.VMEM((DECODE_SLOTS, G * page, MLA_KV_LORA), F32),
                        pltpu.VMEM((DECODE_SLOTS, MLA_ROPE, G * page), F32),
                        pltpu.SemaphoreType.DMA((DECODE_SLOTS, 2)),
                        pltpu.VMEM((MLA_HEADS, 1), F32),
                        pltpu.VMEM((MLA_HEADS, 1), F32),
                        pltpu.VMEM((MLA_HEADS, MLA_KV_LORA), F32)])
    return pl.pallas_call(
        functools.partial(_decode_kernel, layer=layer, G=G),
        grid_spec=grid_spec,
        out_shape=jax.ShapeDtypeStruct((R, MLA_HEADS, MLA_KV_LORA), F32),
        compiler_params=_cparams(("arbitrary", "arbitrary")),
    )(page_table, qlat, qr, ckv_new.reshape(R, 1, MLA_KV_LORA), kr_new.reshape(R, 1, MLA_ROPE),
      cache_ckv, jnp.swapaxes(cache_kr, 2, 3))


def _uv_proj_kernel(o_ref, wuv_ref, y_ref):
    for hh in range(MLA_HEADS):
        y_ref[:, hh * MLA_V:(hh + 1) * MLA_V] = _dot(o_ref[:, hh, :].astype(BF16), wuv_ref[hh])


def _uv_proj(o_lat, w_uv_h):
    R = o_lat.shape[0]
    return pl.pallas_call(
        _uv_proj_kernel,
        grid=(1,),
        in_specs=[_full(o_lat.shape), _full(w_uv_h.shape)],
        out_specs=_full((R, MLA_INNER)),
        out_shape=jax.ShapeDtypeStruct((R, MLA_INNER), F32),
        compiler_params=_cparams(("arbitrary",)),
    )(o_lat, w_uv_h)


def _rope_tables(pos, lane0):
    inv = 1.0 / (ROPE_BASE ** (jnp.arange(ROPE_HALF, dtype=F32) / ROPE_HALF))
    ang = pos.astype(F32)[:, None] * inv[None, :]
    cos, sin = jnp.cos(ang), jnp.sin(ang)
    n = pos.shape[0]
    z = lambda w: jnp.zeros((n, w), F32)
    pre = jnp.ones((n, lane0), F32)
    tail = LANES - lane0 - MLA_ROPE
    rc = jnp.concatenate([pre, cos, cos, z(tail)], 1)
    rs1 = jnp.concatenate([z(lane0), -sin, z(ROPE_HALF), z(tail)], 1)
    rs2 = jnp.concatenate([z(lane0), z(ROPE_HALF), sin, z(tail)], 1)
    return rc, rs1, rs2


def _layer_weights(p, l):
    D = p['w_in'].shape[1]
    w_in = p['w_in'][l]
    offs = [0]
    for n in (SSD_INNER, SSD_CONV_DIM, SSD_HEADS, MLA_Q_LORA, MLA_KV_LORA, MLA_ROPE, MLA_INNER, 2 * D):
        offs.append(offs[-1] + n)
    w_z, w_xbc, w_dt, w_cq, w_ckv, w_kr, w_g, w_gate = [w_in[:, offs[i]:offs[i + 1]] for i in range(8)]
    bf = lambda a: a.astype(BF16)
    pad_head = lambda a: jnp.pad(a, ((0, 0), (0, 0), (0, HEAD_PAD - a.shape[-1])))
    w_uq = p['w_uq'][l].reshape(MLA_Q_LORA, MLA_HEADS, MLA_NOPE + MLA_ROPE)
    w_uk = p['w_uk'][l].reshape(MLA_KV_LORA, MLA_HEADS, MLA_NOPE)
    w_uv = p['w_uv'][l]
    w_uv_pairs = w_uv.reshape(MLA_KV_LORA, MLA_HEADS // 2, 2, MLA_V)
    zv = jnp.zeros((MLA_KV_LORA, MLA_HEADS // 2, MLA_V), F32)
    w_uv_p = jnp.stack([jnp.concatenate([w_uv_pairs[:, :, 0], zv], -1),
                        jnp.concatenate([zv, w_uv_pairs[:, :, 1]], -1)], 2)
    one_at = lambda lane: (jnp.arange(HEAD_PAD) == lane).astype(F32)
    v_ones = jnp.tile(jnp.concatenate([one_at(MLA_V), one_at(0)]), MLA_HEADS // 2)[None]
    w_kr_p = jnp.pad(w_kr, ((0, 0), (ROPE_LANE0, LANES - ROPE_LANE0 - MLA_ROPE)))
    w_uq_rope_p = jnp.pad(w_uq[:, :, MLA_NOPE:], ((0, 0), (0, 0), (ROPE_LANE0, LANES - ROPE_LANE0 - MLA_ROPE)))
    return {
        'norm_w': p['norm_w'][l][None], 'w_xbc': bf(w_xbc), 'conv_w': p['conv_w'][l],
        'conv_b': p['conv_b'][l][None], 'w_dt': bf(w_dt), 'dt_bias': p['dt_bias'][l][None],
        'w_dtT': bf(w_dt.T), 'dt_biasT': p['dt_bias'][l][:, None],
        'w_cq': bf(w_cq), 'q_norm_w': p['q_norm_w'][l][None],
        'w_uq_p': bf(pad_head(w_uq).reshape(MLA_Q_LORA, MLA_HEADS * HEAD_PAD)),
        'w_uq_nope': bf(w_uq[:, :, :MLA_NOPE].reshape(MLA_Q_LORA, MLA_HEADS * MLA_NOPE)),
        'w_uq_rope_p': bf(w_uq_rope_p.reshape(MLA_Q_LORA, MLA_HEADS * LANES)),
        'w_ckv': bf(w_ckv), 'kv_norm_w': p['kv_norm_w'][l][None],
        'w_uk_p': bf(pad_head(w_uk).reshape(MLA_KV_LORA, MLA_HEADS * HEAD_PAD)),
        'w_ukT': bf(jnp.transpose(w_uk, (1, 2, 0))),
        'w_uv_p': bf(w_uv_p.reshape(MLA_KV_LORA, MLA_HEADS * HEAD_PAD)), 'v_ones': v_ones,
        'w_uv_h': bf(jnp.transpose(w_uv.reshape(MLA_KV_LORA, MLA_HEADS, MLA_V), (1, 0, 2))),
        'w_kr_p': bf(w_kr_p),
        'a_log': p['a_log'][l][None], 'a_logT': p['a_log'][l][:, None],
        'd_skip_e': jnp.repeat(p['d_skip'][l], SSD_HEAD_DIM)[None],
        'w_z': bf(w_z), 'w_g': bf(w_g), 'w_gate': bf(w_gate),
        'ssd_norm_w': p['ssd_norm_w'][l][None],
        'w_proj_ssd': bf(p['w_proj_ssd'][l]), 'w_proj_mla': bf(p['w_proj_mla'][l]), 'w_out': bf(p['w_out'][l]),
    }


SEQ_TILE = 512
ATTN_TILE = 256
MERGE_TILE = 512
STEP_ROWS = 8
SSD_CHUNKS_PER_STEP = 2
PAGES_PER_STEP = 32
DECODE_SLOTS = 3


def _seq_layer(x, ctx0, s0T, kpre, vpre, rope_tabs, lw, fnw, final, chunk, need_out=True):
    B, L, D = x.shape
    xbc_c, dt, dtT, q, k, v, ckv, kr, conv_state = _seq_inproj(x, ctx0, rope_tabs, lw, SEQ_TILE)
    y_ssd, sT = _ssd_seq(xbc_c, dt, dtT, s0T, lw, chunk)
    if not need_out:
        return None, (ckv, kr, sT, conv_state, k, v)
    y_mla = _attn_seq(q, k, v, kpre, vpre, ATTN_TILE)
    out = _merge_rows(x.reshape(B * L, D), y_ssd.reshape(B * L, SSD_INNER), y_mla.reshape(B * L, MLA_INNER),
                      lw, fnw, final, MERGE_TILE).reshape(B, L, D)
    return out, (ckv, kr, sT, conv_state, k, v)


def kernel(x_prompt, x_sample, cache_ckv, cache_kr, state_ssm, state_conv, page_table, meta_tokens,
           norm_w, w_in, conv_w, conv_b, dt_bias, a_log, d_skip, ssd_norm_w, q_norm_w, w_uq, kv_norm_w,
           w_uk, w_uv, w_proj_ssd, w_proj_mla, w_out, final_norm_w):
    p = {'norm_w': norm_w, 'w_in': w_in, 'conv_w': conv_w, 'conv_b': conv_b, 'dt_bias': dt_bias,
         'a_log': a_log, 'd_skip': d_skip, 'ssd_norm_w': ssd_norm_w, 'q_norm_w': q_norm_w, 'w_uq': w_uq,
         'kv_norm_w': kv_norm_w, 'w_uk': w_uk, 'w_uv': w_uv, 'w_proj_ssd': w_proj_ssd,
         'w_proj_mla': w_proj_mla, 'w_out': w_out}
    depth = w_in.shape[0]
    B, L, D = x_prompt.shape
    R = x_sample.shape[0]
    n_meta = meta_tokens.shape[0]
    past_len = page_table.shape[1] * cache_ckv.shape[2]
    fnw = final_norm_w[None]

    tabs_meta = _rope_tables(jnp.arange(n_meta), ROPE_LANE0)
    tabs_prompt = _rope_tables(n_meta + jnp.arange(L), ROPE_LANE0)
    tabs_sample = _rope_tables(past_len + jnp.arange(1), ROPE_LANE0)

    hm = meta_tokens[None].astype(x_prompt.dtype)
    hp = x_prompt
    hs = x_sample.reshape(R, D)
    outs = {k: [] for k in ('ckv_p', 'kr_p', 'ssm_p', 'conv_p', 'ckv_s', 'kr_s', 'ssm_s', 'conv_s')}
    for l in range(depth):
        lw = _layer_weights(p, l)
        final = l == depth - 1
        zero_ctx = jnp.zeros((SSD_CONV - 1, SSD_CONV_DIM), F32)
        zero_state = jnp.zeros((SSD_STATE, SSD_INNER), F32)
        hm_next, (ckv_m, kr_m, s_m, cs_m, k_m, v_m) = _seq_layer(
            hm, zero_ctx, zero_state, None, None, tabs_meta, lw, fnw, False, n_meta, need_out=not final)
        hp, (ckv_pp, kr_pp, s_p, cs_p, _, _) = _seq_layer(
            hp, cs_m[0], s_m[0].T, k_m[0], v_m[0], tabs_prompt, lw, fnw, final, SSD_CHUNK)
        hm = hm_next
        outs['ckv_p'].append((ckv_m, ckv_pp))
        outs['kr_p'].append((kr_m, kr_pp))
        outs['ssm_p'].append(s_p.reshape(B, SSD_HEADS, SSD_HEAD_DIM, SSD_STATE))
        outs['conv_p'].append(cs_p)

        ctx = jnp.swapaxes(state_conv[l], 0, 1)
        xbc_c, dt, cs_s, qlat, qr, ckv_s, kr_s = _sample_inproj(hs, ctx, tabs_sample, lw)
        y_ssd, s_new = _ssd_step(xbc_c, dt, state_ssm.reshape(depth, R, SSD_INNER, SSD_STATE), l, lw, STEP_ROWS)
        o_lat = _decode_attn(page_table, qlat, qr, ckv_s, kr_s, cache_ckv, cache_kr, l, PAGES_PER_STEP)
        y_mla = _uv_proj(o_lat, lw['w_uv_h'])
        hs = _merge_rows(hs, y_ssd, y_mla, lw, fnw, final, MERGE_TILE)
        outs['ckv_s'].append(ckv_s.reshape(R, 1, MLA_KV_LORA))
        outs['kr_s'].append(kr_s.reshape(R, 1, MLA_ROPE))
        outs['ssm_s'].append(s_new.reshape(R, SSD_HEADS, SSD_HEAD_DIM, SSD_STATE))
        outs['conv_s'].append(jnp.swapaxes(cs_s, 0, 1))

    st = lambda k: jnp.stack(outs[k])

    def with_meta(k):
        meta = jnp.stack([m for m, _ in outs[k]])
        rows = jnp.stack([r for _, r in outs[k]])
        return jnp.concatenate([jnp.broadcast_to(meta, (depth, B) + meta.shape[2:]), rows], 2)

    return (hp, hs.reshape(R, 1, D), with_meta('ckv_p'), with_meta('kr_p'), st('ssm_p'), st('conv_p'),
            st('ckv_s'), st('kr_s'), st('ssm_s'), st('conv_s'))
```

```python
import functools
import math

import jax
import jax.numpy as jnp
from jax import lax
from jax.experimental import pallas as pl
from jax.experimental.pallas import tpu as pltpu

N_META = 16
NORM_EPS = 1e-6
SSD_HEADS = 16
SSD_HEAD_DIM = 64
SSD_INNER = SSD_HEADS * SSD_HEAD_DIM
SSD_GROUPS = 4
HEADS_PER_GROUP = SSD_HEADS // SSD_GROUPS
SSD_STATE = 128
SSD_CONV = 4
SSD_CHUNK = 128
SSD_BC = SSD_GROUPS * SSD_STATE
SSD_CONV_DIM = SSD_INNER + 2 * SSD_BC
MLA_HEADS = 8
MLA_NOPE = 64
MLA_ROPE = 32
MLA_V = 64
MLA_Q_LORA = 384
MLA_KV_LORA = 256
MLA_INNER = MLA_HEADS * MLA_V
MLA_SCALE = (MLA_NOPE + MLA_ROPE) ** -0.5
ROPE_BASE = 10000.0
LOG2E = math.log2(math.e)

LANES = 128
HEAD_PAD = LANES
ROPE_LANE0 = MLA_NOPE
ROPE_HALF = MLA_ROPE // 2
CTX_ROW0 = 8 - (SSD_CONV - 1)
VMEM_LIMIT = 56 * 1024 * 1024

BF16 = jnp.bfloat16
F32 = jnp.float32


def _cparams(sem):
    return pltpu.CompilerParams(dimension_semantics=sem, vmem_limit_bytes=VMEM_LIMIT)


def _full(shape):
    n = len(shape)
    return pl.BlockSpec(shape, lambda *_: (0,) * n)


def _rms(x, w):
    return x * lax.rsqrt(jnp.mean(x * x, -1, keepdims=True) + NORM_EPS) * w


def _sigmoid(x):
    return 0.5 + 0.5 * jnp.tanh(0.5 * x)


def _silu(x):
    return x * _sigmoid(x)


def _softplus(x):
    return jnp.maximum(x, 0.0) + jnp.log(1.0 + jnp.exp(-jnp.abs(x)))


def _dot(a, b):
    return jnp.dot(a, b, preferred_element_type=F32)


def _dot_nt(a, b):
    return lax.dot_general(a, b, (((1,), (1,)), ((), ())), preferred_element_type=F32)


def _dot_tn(a, b):
    return lax.dot_general(a, b, (((0,), (0,)), ((), ())), preferred_element_type=F32)


def _heads(x):
    return jnp.stack([x[:, hh * HEAD_PAD:(hh + 1) * HEAD_PAD] for hh in range(MLA_HEADS)])


def _bdot_nt(a, b):
    return lax.dot_general(a, b, (((2,), (2,)), ((0,), (0,))), preferred_element_type=F32)


def _bdot(a, b):
    return lax.dot_general(a, b, (((2,), (1,)), ((0,), (0,))), preferred_element_type=F32)


def _split3(a):
    a1 = a.astype(BF16)
    r = a - a1.astype(F32)
    a2 = r.astype(BF16)
    a3 = (r - a2.astype(F32)).astype(BF16)
    return a1, a2, a3


def _dot_sel(a, sel):
    a1, a2, a3 = _split3(a)
    return _dot(a1, sel) + _dot(a2, sel) + _dot(a3, sel)


def _dot_sel2(a, sel):
    a1 = a.astype(BF16)
    a2 = (a - a1.astype(F32)).astype(BF16)
    return _dot(a1, sel) + _dot(a2, sel)


def _sel_dot(sel, a):
    a1, a2, a3 = _split3(a)
    return _dot(sel, a1) + _dot(sel, a2) + _dot(sel, a3)


def _head_expand():
    r = lax.broadcasted_iota(jnp.int32, (SSD_HEADS, SSD_INNER), 0)
    c = lax.broadcasted_iota(jnp.int32, (SSD_HEADS, SSD_INNER), 1)
    return jnp.where(c // SSD_HEAD_DIM == r, 1.0, 0.0).astype(BF16)


def _rope_lanes(x, c, s1, s2):
    n = x.shape[-1]
    return x * c + pltpu.roll(x, n - ROPE_HALF, 1) * s1 + pltpu.roll(x, ROPE_HALF, 1) * s2


def _seq_inproj_kernel(x_ref, normw_ref, wxbc_ref, convw_ref, convb_ref, ctx0_ref,
                       wdt_ref, dtb_ref, wdtT_ref, dtbT_ref,
                       wcq_ref, qnw_ref, wuq_ref, wckv_ref, kvnw_ref, wuk_ref, wuv_ref, vone_ref, wkr_ref,
                       rc_ref, rs1_ref, rs2_ref,
                       xbc_out, dt_out, dtT_out, q_out, k_out, v_out, ckv_out, kr_out, cs_out,
                       rbuf, ybuf, ctxbuf):
    t = pl.program_id(1)
    T = x_ref.shape[1]
    NS, RP, _ = rbuf.shape
    S = RP // 8
    nctx = SSD_CONV - 1
    h = _rms(x_ref[0], normw_ref[...]).astype(BF16)

    @pl.when(t == 0)
    def _():
        ctxbuf[0:nctx, :] = ctx0_ref[...]
        rbuf[:, T:RP, :] = jnp.zeros((NS, RP - T, LANES), F32)

    sub = lax.broadcasted_iota(jnp.int32, (8, LANES), 0)
    for cs in range(NS):
        lanes = slice(cs * LANES, (cs + 1) * LANES)
        if cs % 2 == 0:
            raw2 = _dot(h, wxbc_ref[:, cs * LANES:(cs + 2) * LANES])
            rbuf[cs, 0:T, :] = raw2[:, :LANES]
            rbuf[cs + 1, 0:T, :] = raw2[:, LANES:]
        blocks = [rbuf[cs, pl.ds(v, 8, stride=S), :] for v in range(S)]
        perm = jnp.concatenate(blocks, axis=0)
        wrap = [jnp.where(sub == 0, ctxbuf[i:i + 1, lanes], pltpu.roll(blocks[S - nctx + i], 1, 0))
                for i in range(nctx)]
        acc = convb_ref[:, lanes] + perm * convw_ref[nctx:nctx + 1, lanes]
        for j in range(1, SSD_CONV):
            shifted = jnp.concatenate(wrap[nctx - j:] + [perm[:RP - 8 * j]], axis=0)
            acc = acc + shifted * convw_ref[nctx - j:nctx - j + 1, lanes]
        for v in range(S):
            ybuf[cs, pl.ds(v, 8, stride=S), :] = acc[8 * v:8 * v + 8]
        xbc_out[0, cs] = ybuf[cs, 0:T, :]
    for cs in range(NS):
        last = rbuf[cs, T - nctx:T, :]
        cs_out[0, :, cs * LANES:(cs + 1) * LANES] = last
        ctxbuf[0:nctx, cs * LANES:(cs + 1) * LANES] = last

    dt_out[0] = _softplus(_dot(h, wdt_ref[...]) + dtb_ref[...])
    dtT_out[0] = _softplus(_dot_nt(wdtT_ref[...], h) + dtbT_ref[...])

    rc, rs1, rs2 = rc_ref[...], rs1_ref[...], rs2_ref[...]
    cq = _rms(_dot(h, wcq_ref[...]), qnw_ref[...]).astype(BF16)
    q = _dot(cq, wuq_ref[...])
    q = _rope_lanes(q, jnp.tile(rc, (1, MLA_HEADS)), jnp.tile(rs1, (1, MLA_HEADS)),
                    jnp.tile(rs2, (1, MLA_HEADS)))
    q_out[0] = (q * (MLA_SCALE * LOG2E)).astype(BF16)

    kr = _rope_lanes(_dot(h, wkr_ref[...]), rc, rs1, rs2)
    kr_out[0] = kr[:, ROPE_LANE0:ROPE_LANE0 + MLA_ROPE]
    ckv = _rms(_dot(h, wckv_ref[...]), kvnw_ref[...])
    ckv_out[0] = ckv
    cb = ckv.astype(BF16)
    k_out[0] = (_dot(cb, wuk_ref[...]) + jnp.tile(kr, (1, MLA_HEADS))).astype(BF16)
    v_out[0] = (_dot(cb, wuv_ref[...]) + vone_ref[...]).astype(BF16)


def _seq_inproj(x, ctx0, rope_tabs, lw, tile):
    B, L, D = x.shape
    T = min(tile, L)
    NS = SSD_CONV_DIM // LANES
    stride = T // 8 + 4
    assert stride % 8 != 0 and T % 8 == 0
    rc, rs1, rs2 = rope_tabs
    row = lambda w: pl.BlockSpec((1, T, w), lambda b, t: (b, t, 0))
    tab = pl.BlockSpec((T, LANES), lambda b, t: (t, 0))
    weights = [lw['norm_w'], lw['w_xbc'], lw['conv_w'], lw['conv_b'], ctx0,
               lw['w_dt'], lw['dt_bias'], lw['w_dtT'], lw['dt_biasT'],
               lw['w_cq'], lw['q_norm_w'], lw['w_uq_p'], lw['w_ckv'], lw['kv_norm_w'],
               lw['w_uk_p'], lw['w_uv_p'], lw['v_ones'], lw['w_kr_p']]
    out_shape = (
        jax.ShapeDtypeStruct((B, NS, L, LANES), F32),
        jax.ShapeDtypeStruct((B, L, SSD_HEADS), F32),
        jax.ShapeDtypeStruct((B, SSD_HEADS, L), F32),
        jax.ShapeDtypeStruct((B, L, MLA_HEADS * HEAD_PAD), BF16),
        jax.ShapeDtypeStruct((B, L, MLA_HEADS * HEAD_PAD), BF16),
        jax.ShapeDtypeStruct((B, L, MLA_HEADS * HEAD_PAD), BF16),
        jax.ShapeDtypeStruct((B, L, MLA_KV_LORA), F32),
        jax.ShapeDtypeStruct((B, L, MLA_ROPE), F32),
        jax.ShapeDtypeStruct((B, SSD_CONV - 1, SSD_CONV_DIM), F32),
    )
    out_specs = (pl.BlockSpec((1, NS, T, LANES), lambda b, t: (b, 0, t, 0)), row(SSD_HEADS),
                 pl.BlockSpec((1, SSD_HEADS, T), lambda b, t: (b, 0, t)),
                 row(MLA_HEADS * HEAD_PAD), row(MLA_HEADS * HEAD_PAD), row(MLA_HEADS * HEAD_PAD),
                 row(MLA_KV_LORA), row(MLA_ROPE),
                 pl.BlockSpec((1, SSD_CONV - 1, SSD_CONV_DIM), lambda b, t: (b, 0, 0)))
    return pl.pallas_call(
        _seq_inproj_kernel,
        grid=(B, L // T),
        in_specs=[row(D)] + [_full(w.shape) for w in weights] + [tab, tab, tab],
        out_specs=out_specs,
        out_shape=out_shape,
        scratch_shapes=[pltpu.VMEM((NS, 8 * stride, LANES), F32),
                        pltpu.VMEM((NS, 8 * stride, LANES), F32),
                        pltpu.VMEM((8, SSD_CONV_DIM), F32)],
        compiler_params=_cparams(("parallel", "arbitrary")),
    )(x, *weights, rc, rs1, rs2)


def _ssd_seq_kernel(xbc_ref, dtall_ref, dtTall_ref, dtT_ref, alog_ref, alogT_ref, dskip_ref, s0_ref,
                    y_out, s_out, sT, acs_s, acsT_s, dtw_s, cdec_s):
    c = pl.program_id(1)
    NCH, Q, _ = acs_s.shape
    CPS = xbc_ref.shape[2] // Q
    GW = HEADS_PER_GROUP * SSD_HEAD_DIM
    ri = lax.broadcasted_iota(jnp.int32, (Q, Q), 0)
    ci = lax.broadcasted_iota(jnp.int32, (Q, Q), 1)
    causal = ci <= ri

    @pl.when(c == 0)
    def _():
        sT[...] = s0_ref[...]
        tril = jnp.broadcast_to(jnp.where(causal, 1.0, 0.0).astype(BF16)[None], (NCH, Q, Q))
        triu = jnp.broadcast_to(jnp.where(ri <= ci, 1.0, 0.0).astype(BF16)[None], (NCH, Q, Q))
        dt3 = dtall_ref[0].reshape(NCH, Q, SSD_HEADS)
        a1, a2, a3 = _split3(dt3 * (-jnp.exp(alog_ref[...])))
        acs3 = _bdot(tril, a1) + _bdot(tril, a2) + _bdot(tril, a3)
        aT = dtTall_ref[0] * (-jnp.exp(alogT_ref[...]))
        b1, b2, b3 = _split3(jnp.stack([aT[:, k * Q:(k + 1) * Q] for k in range(NCH)]))
        acs_s[...] = acs3
        acsT_s[...] = _bdot(b1, triu) + _bdot(b2, triu) + _bdot(b3, triu)
        end3 = acs3[:, Q - 1:Q, :]
        dtw_s[...] = dt3 * jnp.exp(end3 - acs3)
        cdec_s[...] = jnp.exp(end3)

    expand = _head_expand()
    lane = lax.broadcasted_iota(jnp.int32, (Q, LANES), 1)
    for k in range(CPS):
        rows = slice(k * Q, (k + 1) * Q)
        ch = c * CPS + k
        acs = acs_s[ch]
        acsT = acsT_s[ch]
        dtT = dtT_ref[0, :, rows]
        dtw_e = _dot_sel2(dtw_s[ch], expand)
        chunk_decay = _dot_sel(cdec_s[ch], expand)
        xs = _silu(jnp.concatenate([xbc_ref[0, cs, rows, :] for cs in range(SSD_INNER // LANES)], axis=1))
        xs_b = xs.astype(BF16)
        xw = (xs * dtw_e).astype(BF16)
        dxs = xs * dskip_ref[...]

        for g in range(SSD_GROUPS):
            lo = g * GW
            Bg = _silu(xbc_ref[0, SSD_INNER // LANES + g, rows, :]).astype(BF16)
            Cg_f = _silu(xbc_ref[0, (SSD_INNER + SSD_BC) // LANES + g, rows, :])
            cb = _dot_nt(Cg_f.astype(BF16), Bg)
            heads = range(g * HEADS_PER_GROUP, (g + 1) * HEADS_PER_GROUP)
            acs_b = jnp.stack([jnp.broadcast_to(acs[:, hh:hh + 1], (Q, LANES)) for hh in heads])
            acs_r = jnp.stack([acsT[hh:hh + 1, :] for hh in heads])
            dt_r = jnp.stack([dtT[hh:hh + 1, :] for hh in heads])
            m = cb[None] * jnp.exp(jnp.where(causal[None], acs_b[:, :, :Q] - acs_r, -jnp.inf)) * dt_r
            lhs = jnp.concatenate([Cg_f[None] * jnp.exp(acs_b), m], axis=2).astype(BF16)
            pair_rhs = [jnp.concatenate([sT[:, lo + pr * LANES:lo + (pr + 1) * LANES].astype(BF16),
                                         xs_b[:, lo + pr * LANES:lo + (pr + 1) * LANES]], axis=0)
                        for pr in range(HEADS_PER_GROUP // 2)]
            y = _bdot(lhs, jnp.stack([pair_rhs[e // 2] for e in range(HEADS_PER_GROUP)]))
            for pr in range(HEADS_PER_GROUP // 2):
                plo = lo + pr * LANES
                y_out[0, rows, plo:plo + LANES] = (jnp.where(lane < SSD_HEAD_DIM, y[2 * pr], y[2 * pr + 1])
                                                   + dxs[:, plo:plo + LANES])
            sT[:, lo:lo + GW] = sT[:, lo:lo + GW] * chunk_decay[:, lo:lo + GW] + _dot_tn(Bg, xw[:, lo:lo + GW])

    @pl.when(c == pl.num_programs(1) - 1)
    def _():
        for t in range(SSD_INNER // LANES):
            s_out[0, t * LANES:(t + 1) * LANES, :] = sT[:, t * LANES:(t + 1) * LANES].T


def _ssd_seq(xbc_c, dt, dtT, s0T, lw, chunk):
    B, NS, L, _ = xbc_c.shape
    Q = min(chunk, L)
    R = Q * min(SSD_CHUNKS_PER_STEP, L // Q)
    assert SSD_STATE == LANES
    return pl.pallas_call(
        _ssd_seq_kernel,
        grid=(B, L // R),
        in_specs=[pl.BlockSpec((1, NS, R, LANES), lambda b, c: (b, 0, c, 0)),
                  pl.BlockSpec((1, L, SSD_HEADS), lambda b, c: (b, 0, 0)),
                  pl.BlockSpec((1, SSD_HEADS, L), lambda b, c: (b, 0, 0)),
                  pl.BlockSpec((1, SSD_HEADS, R), lambda b, c: (b, 0, c)),
                  _full((1, SSD_HEADS)), _full((SSD_HEADS, 1)), _full((1, SSD_INNER)),
                  _full((SSD_STATE, SSD_INNER))],
        out_specs=(pl.BlockSpec((1, R, SSD_INNER), lambda b, c: (b, c, 0)),
                   pl.BlockSpec((1, SSD_INNER, SSD_STATE), lambda b, c: (b, 0, 0))),
        out_shape=(jax.ShapeDtypeStruct((B, L, SSD_INNER), F32),
                   jax.ShapeDtypeStruct((B, SSD_INNER, SSD_STATE), F32)),
        scratch_shapes=[pltpu.VMEM((SSD_STATE, SSD_INNER), F32),
                        pltpu.VMEM((L // Q, Q, SSD_HEADS), F32),
                        pltpu.VMEM((L // Q, SSD_HEADS, Q), F32),
                        pltpu.VMEM((L // Q, Q, SSD_HEADS), F32),
                        pltpu.VMEM((L // Q, 1, SSD_HEADS), F32)],
        compiler_params=_cparams(("parallel", "arbitrary")),
    )(xbc_c, dt, dtT, dtT, lw['a_log'], lw['a_logT'], lw['d_skip_e'], s0T)


def _attn_seq_kernel(*refs, n_pre, single_tile):
    if n_pre:
        q_ref, k_ref, v_ref, kpre_ref, vpre_ref, o_ref, m_s, acc_s = refs
    else:
        q_ref, k_ref, v_ref, o_ref, m_s, acc_s = refs
    i = pl.program_id(1)
    TQ = q_ref.shape[1]
    TKB = 2 * TQ
    causal = (lax.broadcasted_iota(jnp.int32, (TQ, TQ), 1)
              <= lax.broadcasted_iota(jnp.int32, (TQ, TQ), 0))[None]
    mshape = (MLA_HEADS, TQ, HEAD_PAD)
    m_s[...] = jnp.full(mshape, -jnp.inf, F32)
    acc_s[...] = jnp.zeros(mshape, F32)

    def lanes_of(m, n):
        return m[..., :n] if n <= HEAD_PAD else jnp.concatenate([m] * (n // HEAD_PAD), -1)

    def update(s, vs):
        m_old = m_s[...]
        m_new = jnp.maximum(m_old, jnp.max(s, -1, keepdims=True))
        p = jnp.exp2(s - lanes_of(m_new, s.shape[-1])).astype(BF16)
        acc_s[...] = acc_s[...] * jnp.exp2(m_old - m_new) + _bdot(p, vs)
        m_s[...] = m_new

    def body(j, carry):
        r = pl.multiple_of(j * TKB, TKB)
        update(_bdot_nt(_heads(q_ref[0]), _heads(k_ref[0, pl.ds(r, TKB), :])),
               _heads(v_ref[0, pl.ds(r, TKB), :]))
        return carry

    lax.fori_loop(0, lax.div(i, 2), body, 0)

    def tail(with_full):
        q3 = _heads(q_ref[0])
        ss, vv = [], []
        if n_pre:
            col = lax.broadcasted_iota(jnp.int32, (TQ, kpre_ref.shape[0]), 1)
            ss.append(jnp.where((col < n_pre)[None], _bdot_nt(q3, _heads(kpre_ref[...])), -jnp.inf))
            vv.append(_heads(vpre_ref[...]))
        r_diag = pl.multiple_of(i * TQ, TQ)
        if with_full:
            r_full = pl.multiple_of(r_diag - TQ, TQ)
            ss.append(_bdot_nt(q3, _heads(k_ref[0, pl.ds(r_full, TQ), :])))
            vv.append(_heads(v_ref[0, pl.ds(r_full, TQ), :]))
        ss.append(jnp.where(causal, _bdot_nt(q3, _heads(k_ref[0, pl.ds(r_diag, TQ), :])), -jnp.inf))
        vv.append(_heads(v_ref[0, pl.ds(r_diag, TQ), :]))
        update(jnp.concatenate(ss, -1), jnp.concatenate(vv, 1))

    if single_tile:
        tail(False)
    else:
        pl.when(lax.rem(i, 2) == 0)(lambda: tail(False))
        pl.when(lax.rem(i, 2) == 1)(lambda: tail(True))

    lane = lax.broadcasted_iota(jnp.int32, (TQ, HEAD_PAD), 1)
    for hp in range(MLA_HEADS // 2):
        a0, a1 = acc_s[2 * hp], acc_s[2 * hp + 1]
        o0 = a0 * (1.0 / a0[:, MLA_V:MLA_V + 1])
        o1 = a1 * (1.0 / a1[:, 0:1])
        o_ref[0, :, hp * HEAD_PAD:(hp + 1) * HEAD_PAD] = jnp.where(lane < MLA_V, o0, o1)


def _attn_seq(q, k, v, kpre, vpre, tile):
    B, L, W = q.shape
    TQ = min(tile, L)
    n_pre = 0 if kpre is None else kpre.shape[0]
    seq = pl.BlockSpec((1, L, W), lambda b, i: (b, 0, 0))
    in_specs = [pl.BlockSpec((1, TQ, W), lambda b, i: (b, i, 0)), seq, seq]
    args = [q, k, v]
    if n_pre:
        pad = ((0, HEAD_PAD - n_pre), (0, 0))
        kpre, vpre = jnp.pad(kpre, pad), jnp.pad(vpre, pad)
        in_specs += [_full(kpre.shape), _full(vpre.shape)]
        args += [kpre, vpre]
    return pl.pallas_call(
        functools.partial(_attn_seq_kernel, n_pre=n_pre, single_tile=(L == TQ)),
        grid=(B, L // TQ),
        in_specs=in_specs,
        out_specs=pl.BlockSpec((1, TQ, MLA_INNER), lambda b, i: (b, i, 0)),
        out_shape=jax.ShapeDtypeStruct((B, L, MLA_INNER), F32),
        scratch_shapes=[pltpu.VMEM((MLA_HEADS, TQ, HEAD_PAD), F32),
                        pltpu.VMEM((MLA_HEADS, TQ, HEAD_PAD), F32)],
        compiler_params=_cparams(("parallel", "arbitrary")),
    )(*args)


def _merge_kernel(x_ref, yssd_ref, ymla_ref, normw_ref, wz_ref, wg_ref, wgate_ref, ssdnw_ref,
                  wps_ref, wpm_ref, wo_ref, fnw_ref, o_ref, *, final):
    x = x_ref[...]
    h = _rms(x, normw_ref[...]).astype(BF16)
    z = _dot(h, wz_ref[...])
    y_ssd = _rms(yssd_ref[...] * _silu(z), ssdnw_ref[...]).astype(BF16)
    y_mla = (ymla_ref[...] * _silu(_dot(h, wg_ref[...]))).astype(BF16)
    gate = _sigmoid(_dot(h, wgate_ref[...]))
    D = x.shape[-1]
    u = gate[:, :D] * _dot(y_ssd, wps_ref[...]) + gate[:, D:] * _dot(y_mla, wpm_ref[...])
    out = x + _dot(u.astype(BF16), wo_ref[...])
    if final:
        out = _rms(out, fnw_ref[...])
    o_ref[...] = out


def _merge_rows(x, y_ssd, y_mla, lw, fnw, final, tile):
    R, D = x.shape
    T = min(tile, R)
    row = lambda w: pl.BlockSpec((T, w), lambda r: (r, 0))
    weights = [lw['norm_w'], lw['w_z'], lw['w_g'], lw['w_gate'], lw['ssd_norm_w'],
               lw['w_proj_ssd'], lw['w_proj_mla'], lw['w_out'], fnw]
    return pl.pallas_call(
        functools.partial(_merge_kernel, final=final),
        grid=(R // T,),
        in_specs=[row(D), row(SSD_INNER), row(MLA_INNER)] + [_full(w.shape) for w in weights],
        out_specs=row(D),
        out_shape=jax.ShapeDtypeStruct((R, D), F32),
        compiler_params=_cparams(("parallel",)),
    )(x, y_ssd, y_mla, *weights)


def _sample_inproj_kernel(x_ref, normw_ref, wxbc_ref, convw_ref, convb_ref, ctx_ref,
                          wdt_ref, dtb_ref, wcq_ref, qnw_ref, wuqn_ref, wuqr_ref, wukT_ref,
                          wckv_ref, kvnw_ref, wkr_ref, rc_ref, rs1_ref, rs2_ref,
                          xbc_out, dt_out, cs_out, qlat_out, qr_out, ckv_out, kr_out):
    C = SSD_CONV_DIM
    h = _rms(x_ref[...], normw_ref[...]).astype(BF16)
    raw = _dot(h, wxbc_ref[...])
    acc = convb_ref[...] + raw * convw_ref[SSD_CONV - 1:SSD_CONV, :]
    for k in range(SSD_CONV - 1):
        acc = acc + ctx_ref[k] * convw_ref[k:k + 1, :]
    xbc_out[...] = _silu(acc)
    for k in range(SSD_CONV - 2):
        cs_out[k] = ctx_ref[k + 1]
    cs_out[SSD_CONV - 2] = raw
    dt_out[...] = _softplus(_dot(h, wdt_ref[...]) + dtb_ref[...])

    rc, rs1, rs2 = rc_ref[...], rs1_ref[...], rs2_ref[...]
    cq = _rms(_dot(h, wcq_ref[...]), qnw_ref[...]).astype(BF16)
    qscale = MLA_SCALE * LOG2E
    qn = (_dot(cq, wuqn_ref[...]) * qscale).astype(BF16)
    qr = _rope_lanes(_dot(cq, wuqr_ref[...]), jnp.tile(rc, (1, MLA_HEADS)),
                     jnp.tile(rs1, (1, MLA_HEADS)), jnp.tile(rs2, (1, MLA_HEADS))) * qscale
    for hh in range(MLA_HEADS):
        qlat_out[:, hh, :] = _dot(qn[:, hh * MLA_NOPE:(hh + 1) * MLA_NOPE], wukT_ref[hh])
        qr_out[:, hh, :] = qr[:, hh * LANES + ROPE_LANE0:hh * LANES + ROPE_LANE0 + MLA_ROPE]
    kr = _rope_lanes(_dot(h, wkr_ref[...]), rc, rs1, rs2)
    kr_out[...] = kr[:, ROPE_LANE0:ROPE_LANE0 + MLA_ROPE]
    ckv_out[...] = _rms(_dot(h, wckv_ref[...]), kvnw_ref[...])


def _sample_inproj(x, ctx, rope_tabs, lw):
    R, D = x.shape
    rc, rs1, rs2 = rope_tabs
    args = [x, lw['norm_w'], lw['w_xbc'], lw['conv_w'], lw['conv_b'], ctx, lw['w_dt'], lw['dt_bias'],
            lw['w_cq'], lw['q_norm_w'], lw['w_uq_nope'], lw['w_uq_rope_p'], lw['w_ukT'],
            lw['w_ckv'], lw['kv_norm_w'], lw['w_kr_p'], rc, rs1, rs2]
    out_shape = (jax.ShapeDtypeStruct((R, SSD_CONV_DIM), F32),
                 jax.ShapeDtypeStruct((R, SSD_HEADS), F32),
                 jax.ShapeDtypeStruct((SSD_CONV - 1, R, SSD_CONV_DIM), F32),
                 jax.ShapeDtypeStruct((R, MLA_HEADS, MLA_KV_LORA), F32),
                 jax.ShapeDtypeStruct((R, MLA_HEADS, MLA_ROPE), F32),
                 jax.ShapeDtypeStruct((R, MLA_KV_LORA), F32),
                 jax.ShapeDtypeStruct((R, MLA_ROPE), F32))
    return pl.pallas_call(
        _sample_inproj_kernel,
        grid=(1,),
        in_specs=[_full(a.shape) for a in args],
        out_specs=tuple(_full(s.shape) for s in out_shape),
        out_shape=out_shape,
        compiler_params=_cparams(("arbitrary",)),
    )(*args)


def _ssd_step_kernel(xbc_ref, dt_ref, alog_ref, dskip_ref, s_ref, y_out, s_out):
    NB = xbc_ref.shape[0]
    GW = HEADS_PER_GROUP * SSD_HEAD_DIM
    expand = _head_expand()
    dt_e = _dot_sel(dt_ref[...], expand)
    a_e = _dot_sel(-jnp.exp(alog_ref[...]), expand)
    xs = xbc_ref[:, 0:SSD_INNER]
    xdt = xs * dt_e
    dec = jnp.exp(dt_e * a_e)
    dxs = xs * dskip_ref[...]
    KR = 16
    row = lax.broadcasted_iota(jnp.int32, (KR, SSD_INNER), 0)
    col = lax.broadcasted_iota(jnp.int32, (KR, SSD_INNER), 1)
    gmask = jnp.where((row < 2 * SSD_GROUPS) & (col // GW == row % SSD_GROUPS), 1.0, 0.0)
    rown = lax.broadcasted_iota(jnp.int32, (KR, SSD_STATE), 0)
    dec_rows = jnp.where((rown >= 2 * SSD_GROUPS) & (rown < 2 * SSD_GROUPS + 3), 1.0, 0.0)
    for b in range(NB):
        xb = xdt[b:b + 1, :]
        x1 = xb.astype(BF16).astype(F32)
        d1, d2, d3 = [d.astype(F32) for d in _split3(dec[b:b + 1, :])]
        lhs = jnp.where(row < SSD_GROUPS, x1, xb - x1) * gmask
        lhs = lhs + jnp.where(row == 2 * SSD_GROUPS, d1,
                              jnp.where(row == 2 * SSD_GROUPS + 1, d2,
                                        jnp.where(row == 2 * SSD_GROUPS + 2, d3, 0.0)))
        bmat = jnp.zeros((KR, SSD_STATE), F32)
        cmat = jnp.zeros((KR, SSD_STATE), F32)
        for g in range(SSD_GROUPS):
            bg = xbc_ref[b:b + 1, SSD_INNER + g * SSD_STATE:SSD_INNER + (g + 1) * SSD_STATE]
            cg = xbc_ref[b:b + 1, SSD_INNER + SSD_BC + g * SSD_STATE:SSD_INNER + SSD_BC + (g + 1) * SSD_STATE]
            bmat = bmat + jnp.where((rown < 2 * SSD_GROUPS) & (rown % SSD_GROUPS == g), bg, 0.0)
            cmat = cmat + jnp.where(rown == g, cg, 0.0)
        rhs = jnp.concatenate([bmat, dec_rows], axis=1).astype(BF16)
        r = _dot_tn(lhs.astype(BF16), rhs)
        s_new = s_ref[b] * r[:, SSD_STATE:] + r[:, :SSD_STATE]
        s_out[b] = s_new
        yt = _dot_nt(cmat.astype(BF16), s_new.astype(BF16))
        y_out[b:b + 1, :] = jnp.sum(yt * gmask, axis=0, keepdims=True) + dxs[b:b + 1, :]


def _ssd_step(xbc_c, dt, state, layer, lw, nb):
    R = xbc_c.shape[0]
    return pl.pallas_call(
        _ssd_step_kernel,
        grid=(R // nb,),
        in_specs=[pl.BlockSpec((nb, SSD_CONV_DIM), lambda i: (i, 0)),
                  pl.BlockSpec((nb, SSD_HEADS), lambda i: (i, 0)),
                  _full((1, SSD_HEADS)), _full((1, SSD_INNER)),
                  pl.BlockSpec((None, nb, SSD_INNER, SSD_STATE), lambda i: (layer, i, 0, 0))],
        out_specs=(pl.BlockSpec((nb, SSD_INNER), lambda i: (i, 0)),
                   pl.BlockSpec((nb, SSD_INNER, SSD_STATE), lambda i: (i, 0, 0))),
        out_shape=(jax.ShapeDtypeStruct((R, SSD_INNER), F32),
                   jax.ShapeDtypeStruct((R, SSD_INNER, SSD_STATE), F32)),
        compiler_params=_cparams(("parallel",)),
    )(xbc_c, dt, lw['a_log'], lw['d_skip_e'], state)


def _decode_kernel(pt_ref, qlat_ref, qr_ref, ckvn_ref, krn_ref, cckv_hbm, ckr_hbm, o_ref,
                   ckv_buf, kr_buf, sems, m_s, l_s, acc_s, *, layer, G):
    b = pl.program_id(0)
    c = pl.program_id(1)
    NB = pl.num_programs(0)
    NC = pl.num_programs(1)
    step = b * NC + c
    n_slots = ckv_buf.shape[0]
    ahead = n_slots - 1
    slot = lax.rem(step, n_slots)

    P = cckv_hbm.shape[2]

    def start_page(chunk, g):
        sl = lax.rem(chunk, n_slots)
        page = pt_ref[lax.div(chunk, NC), lax.rem(chunk, NC) * G + g]
        r = pl.multiple_of(g * P, P)
        pltpu.make_async_copy(cckv_hbm.at[layer, page], ckv_buf.at[sl, pl.ds(r, P)], sems.at[sl, 0]).start()
        pltpu.make_async_copy(ckr_hbm.at[layer, page], kr_buf.at[sl, :, pl.ds(r, P)], sems.at[sl, 1]).start()

    @pl.when(step == 0)
    def _():
        def prime(i, carry):
            @pl.when(lax.div(i, G) < NB * NC)
            def _():
                start_page(lax.div(i, G), lax.rem(i, G))
            return carry
        lax.fori_loop(0, ahead * G, prime, 0)

    pltpu.make_async_copy(ckv_buf.at[slot], ckv_buf.at[slot], sems.at[slot, 0]).wait()
    pltpu.make_async_copy(kr_buf.at[slot], kr_buf.at[slot], sems.at[slot, 1]).wait()

    @pl.when(step + ahead < NB * NC)
    def _():
        for g in range(G):
            start_page(step + ahead, g)

    @pl.when(c == 0)
    def _():
        m_s[...] = jnp.full(m_s.shape, -jnp.inf, F32)
        l_s[...] = jnp.zeros(l_s.shape, F32)
        acc_s[...] = jnp.zeros(acc_s.shape, F32)

    ql = qlat_ref[0].astype(BF16)
    qr = qr_ref[0].astype(BF16)
    kc = ckv_buf[slot].astype(BF16)
    s = _dot_nt(ql, kc) + _dot(qr, kr_buf[slot].astype(BF16))
    m, l, acc = m_s[...], l_s[...], acc_s[...]
    n_sub = 2 if G % 2 == 0 else 1
    W = G * P // n_sub
    for u in range(n_sub):
        su = s[:, u * W:(u + 1) * W]
        m_new = jnp.maximum(m, jnp.max(su, -1, keepdims=True))
        corr = jnp.exp2(m - m_new)
        p = jnp.exp2(su - m_new)
        l = l * corr + jnp.sum(p, -1, keepdims=True)
        acc = acc * corr + _dot(p.astype(BF16), kc[u * W:(u + 1) * W])
        m = m_new
    m_s[...], l_s[...], acc_s[...] = m, l, acc

    @pl.when(c == NC - 1)
    def _():
        kn = ckvn_ref[0]
        s1 = (jnp.sum(qlat_ref[0] * kn, -1, keepdims=True)
              + jnp.sum(qr_ref[0] * krn_ref[0], -1, keepdims=True))
        m2 = jnp.maximum(m, s1)
        corr2 = jnp.exp2(m - m2)
        p1 = jnp.exp2(s1 - m2)
        o_ref[0] = (acc * corr2 + p1 * kn) * (1.0 / (l * corr2 + p1))


def _decode_attn(page_table, qlat, qr, ckv_new, kr_new, cache_ckv, cache_kr, layer, G):
    R, n_pages = page_table.shape
    page = cache_ckv.shape[2]
    G = min(G, n_pages)
    grid_spec = pltpu.PrefetchScalarGridSpec(
        num_scalar_prefetch=1,
        grid=(R, n_pages // G),
        in_specs=[pl.BlockSpec((1, MLA_HEADS, MLA_KV_LORA), lambda b, c, pt: (b, 0, 0)),
                  pl.BlockSpec((1, MLA_HEADS, MLA_ROPE), lambda b, c, pt: (b, 0, 0)),
                  pl.BlockSpec((1, 1, MLA_KV_LORA), lambda b, c, pt: (b, 0, 0)),
                  pl.BlockSpec((1, 1, MLA_ROPE), lambda b, c, pt: (b, 0, 0)),
                  pl.BlockSpec(memory_space=pl.ANY),
                  pl.BlockSpec(memory_space=pl.ANY)],
        out_specs=pl.BlockSpec((1, MLA_HEADS, MLA_KV_LORA), lambda b, c, pt: (b, 0, 0)),
        scratch_shapes=[pltpu.VMEM((DECODE_SLOTS, G * page, MLA_KV_LORA), F32),
                        pltpu.VMEM((DECODE_SLOTS, MLA_ROPE, G * page), F32),
                        pltpu.SemaphoreType.DMA((DECODE_SLOTS, 2)),
                        pltpu.VMEM((MLA_HEADS, 1), F32),
                        pltpu.VMEM((MLA_HEADS, 1), F32),
                        pltpu.VMEM((MLA_HEADS, MLA_KV_LORA), F32)])
    return pl.pallas_call(
        functools.partial(_decode_kernel, layer=layer, G=G),
        grid_spec=grid_spec,
        out_shape=jax.ShapeDtypeStruct((R, MLA_HEADS, MLA_KV_LORA), F32),
        compiler_params=_cparams(("arbitrary", "arbitrary")),
    )(page_table, qlat, qr, ckv_new.reshape(R, 1, MLA_KV_LORA), kr_new.reshape(R, 1, MLA_ROPE),
      cache_ckv, jnp.swapaxes(cache_kr, 2, 3))


def _uv_proj_kernel(o_ref, wuv_ref, y_ref):
    for hh in range(MLA_HEADS):
        y_ref[:, hh * MLA_V:(hh + 1) * MLA_V] = _dot(o_ref[:, hh, :].astype(BF16), wuv_ref[hh])


def _uv_proj(o_lat, w_uv_h):
    R = o_lat.shape[0]
    return pl.pallas_call(
        _uv_proj_kernel,
        grid=(1,),
        in_specs=[_full(o_lat.shape), _full(w_uv_h.shape)],
        out_specs=_full((R, MLA_INNER)),
        out_shape=jax.ShapeDtypeStruct((R, MLA_INNER), F32),
        compiler_params=_cparams(("arbitrary",)),
    )(o_lat, w_uv_h)


def _rope_tables(pos, lane0):
    inv = 1.0 / (ROPE_BASE ** (jnp.arange(ROPE_HALF, dtype=F32) / ROPE_HALF))
    ang = pos.astype(F32)[:, None] * inv[None, :]
    cos, sin = jnp.cos(ang), jnp.sin(ang)
    n = pos.shape[0]
    z = lambda w: jnp.zeros((n, w), F32)
    pre = jnp.ones((n, lane0), F32)
    tail = LANES - lane0 - MLA_ROPE
    rc = jnp.concatenate([pre, cos, cos, z(tail)], 1)
    rs1 = jnp.concatenate([z(lane0), -sin, z(ROPE_HALF), z(tail)], 1)
    rs2 = jnp.concatenate([z(lane0), z(ROPE_HALF), sin, z(tail)], 1)
    return rc, rs1, rs2


def _layer_weights(p, l):
    D = p['w_in'].shape[1]
    w_in = p['w_in'][l]
    offs = [0]
    for n in (SSD_INNER, SSD_CONV_DIM, SSD_HEADS, MLA_Q_LORA, MLA_KV_LORA, MLA_ROPE, MLA_INNER, 2 * D):
        offs.append(offs[-1] + n)
    w_z, w_xbc, w_dt, w_cq, w_ckv, w_kr, w_g, w_gate = [w_in[:, offs[i]:offs[i + 1]] for i in range(8)]
    bf = lambda a: a.astype(BF16)
    pad_head = lambda a: jnp.pad(a, ((0, 0), (0, 0), (0, HEAD_PAD - a.shape[-1])))
    w_uq = p['w_uq'][l].reshape(MLA_Q_LORA, MLA_HEADS, MLA_NOPE + MLA_ROPE)
    w_uk = p['w_uk'][l].reshape(MLA_KV_LORA, MLA_HEADS, MLA_NOPE)
    w_uv = p['w_uv'][l]
    w_uv_pairs = w_uv.reshape(MLA_KV_LORA, MLA_HEADS // 2, 2, MLA_V)
    zv = jnp.zeros((MLA_KV_LORA, MLA_HEADS // 2, MLA_V), F32)
    w_uv_p = jnp.stack([jnp.concatenate([w_uv_pairs[:, :, 0], zv], -1),
                        jnp.concatenate([zv, w_uv_pairs[:, :, 1]], -1)], 2)
    one_at = lambda lane: (jnp.arange(HEAD_PAD) == lane).astype(F32)
    v_ones = jnp.tile(jnp.concatenate([one_at(MLA_V), one_at(0)]), MLA_HEADS // 2)[None]
    w_kr_p = jnp.pad(w_kr, ((0, 0), (ROPE_LANE0, LANES - ROPE_LANE0 - MLA_ROPE)))
    w_uq_rope_p = jnp.pad(w_uq[:, :, MLA_NOPE:], ((0, 0), (0, 0), (ROPE_LANE0, LANES - ROPE_LANE0 - MLA_ROPE)))
    return {
        'norm_w': p['norm_w'][l][None], 'w_xbc': bf(w_xbc), 'conv_w': p['conv_w'][l],
        'conv_b': p['conv_b'][l][None], 'w_dt': bf(w_dt), 'dt_bias': p['dt_bias'][l][None],
        'w_dtT': bf(w_dt.T), 'dt_biasT': p['dt_bias'][l][:, None],
        'w_cq': bf(w_cq), 'q_norm_w': p['q_norm_w'][l][None],
        'w_uq_p': bf(pad_head(w_uq).reshape(MLA_Q_LORA, MLA_HEADS * HEAD_PAD)),
        'w_uq_nope': bf(w_uq[:, :, :MLA_NOPE].reshape(MLA_Q_LORA, MLA_HEADS * MLA_NOPE)),
        'w_uq_rope_p': bf(w_uq_rope_p.reshape(MLA_Q_LORA, MLA_HEADS * LANES)),
        'w_ckv': bf(w_ckv), 'kv_norm_w': p['kv_norm_w'][l][None],
        'w_uk_p': bf(pad_head(w_uk).reshape(MLA_KV_LORA, MLA_HEADS * HEAD_PAD)),
        'w_ukT': bf(jnp.transpose(w_uk, (1, 2, 0))),
        'w_uv_p': bf(w_uv_p.reshape(MLA_KV_LORA, MLA_HEADS * HEAD_PAD)), 'v_ones': v_ones,
        'w_uv_h': bf(jnp.transpose(w_uv.reshape(MLA_KV_LORA, MLA_HEADS, MLA_V), (1, 0, 2))),
        'w_kr_p': bf(w_kr_p),
        'a_log': p['a_log'][l][None], 'a_logT': p['a_log'][l][:, None],
        'd_skip_e': jnp.repeat(p['d_skip'][l], SSD_HEAD_DIM)[None],
        'w_z': bf(w_z), 'w_g': bf(w_g), 'w_gate': bf(w_gate),
        'ssd_norm_w': p['ssd_norm_w'][l][None],
        'w_proj_ssd': bf(p['w_proj_ssd'][l]), 'w_proj_mla': bf(p['w_proj_mla'][l]), 'w_out': bf(p['w_out'][l]),
    }


SEQ_TILE = 512
ATTN_TILE = 256
MERGE_TILE = 512
STEP_ROWS = 8
SSD_CHUNKS_PER_STEP = 2
PAGES_PER_STEP = 32
DECODE_SLOTS = 3


def _seq_layer(x, ctx0, s0T, kpre, vpre, rope_tabs, lw, fnw, final, chunk, need_out=True):
    B, L, D = x.shape
    xbc_c, dt, dtT, q, k, v, ckv, kr, conv_state = _seq_inproj(x, ctx0, rope_tabs, lw, SEQ_TILE)
    y_ssd, sT = _ssd_seq(xbc_c, dt, dtT, s0T, lw, chunk)
    if not need_out:
        return None, (ckv, kr, sT, conv_state, k, v)
    y_mla = _attn_seq(q, k, v, kpre, vpre, ATTN_TILE)
    out = _merge_rows(x.reshape(B * L, D), y_ssd.reshape(B * L, SSD_INNER), y_mla.reshape(B * L, MLA_INNER),
                      lw, fnw, final, MERGE_TILE).reshape(B, L, D)
    return out, (ckv, kr, sT, conv_state, k, v)


def kernel(x_prompt, x_sample, cache_ckv, cache_kr, state_ssm, state_conv, page_table, meta_tokens,
           norm_w, w_in, conv_w, conv_b, dt_bias, a_log, d_skip, ssd_norm_w, q_norm_w, w_uq, kv_norm_w,
           w_uk, w_uv, w_proj_ssd, w_proj_mla, w_out, final_norm_w):
    p = {'norm_w': norm_w, 'w_in': w_in, 'conv_w': conv_w, 'conv_b': conv_b, 'dt_bias': dt_bias,
         'a_log': a_log, 'd_skip': d_skip, 'ssd_norm_w': ssd_norm_w, 'q_norm_w': q_norm_w, 'w_uq': w_uq,
         'kv_norm_w': kv_norm_w, 'w_uk': w_uk, 'w_uv': w_uv, 'w_proj_ssd': w_proj_ssd,
         'w_proj_mla': w_proj_mla, 'w_out': w_out}
    depth = w_in.shape[0]
    B, L, D = x_prompt.shape
    R = x_sample.shape[0]
    n_meta = meta_tokens.shape[0]
    past_len = page_table.shape[1] * cache_ckv.shape[2]
    fnw = final_norm_w[None]

    tabs_meta = _rope_tables(jnp.arange(n_meta), ROPE_LANE0)
    tabs_prompt = _rope_tables(n_meta + jnp.arange(L), ROPE_LANE0)
    tabs_sample = _rope_tables(past_len + jnp.arange(1), ROPE_LANE0)

    hm = meta_tokens[None].astype(x_prompt.dtype)
    hp = x_prompt
    hs = x_sample.reshape(R, D)
    outs = {k: [] for k in ('ckv_p', 'kr_p', 'ssm_p', 'conv_p', 'ckv_s', 'kr_s', 'ssm_s', 'conv_s')}
    for l in range(depth):
        lw = _layer_weights(p, l)
        final = l == depth - 1
        zero_ctx = jnp.zeros((SSD_CONV - 1, SSD_CONV_DIM), F32)
        zero_state = jnp.zeros((SSD_STATE, SSD_INNER), F32)
        hm_next, (ckv_m, kr_m, s_m, cs_m, k_m, v_m) = _seq_layer(
            hm, zero_ctx, zero_state, None, None, tabs_meta, lw, fnw, False, n_meta, need_out=not final)
        hp, (ckv_pp, kr_pp, s_p, cs_p, _, _) = _seq_layer(
            hp, cs_m[0], s_m[0].T, k_m[0], v_m[0], tabs_prompt, lw, fnw, final, SSD_CHUNK)
        hm = hm_next
        outs['ckv_p'].append((ckv_m, ckv_pp))
        outs['kr_p'].append((kr_m, kr_pp))
        outs['ssm_p'].append(s_p.reshape(B, SSD_HEADS, SSD_HEAD_DIM, SSD_STATE))
        outs['conv_p'].append(cs_p)

        ctx = jnp.swapaxes(state_conv[l], 0, 1)
        xbc_c, dt, cs_s, qlat, qr, ckv_s, kr_s = _sample_inproj(hs, ctx, tabs_sample, lw)
        y_ssd, s_new = _ssd_step(xbc_c, dt, state_ssm.reshape(depth, R, SSD_INNER, SSD_STATE), l, lw, STEP_ROWS)
        o_lat = _decode_attn(page_table, qlat, qr, ckv_s, kr_s, cache_ckv, cache_kr, l, PAGES_PER_STEP)
        y_mla = _uv_proj(o_lat, lw['w_uv_h'])
        hs = _merge_rows(hs, y_ssd, y_mla, lw, fnw, final, MERGE_TILE)
        outs['ckv_s'].append(ckv_s.reshape(R, 1, MLA_KV_LORA))
        outs['kr_s'].append(kr_s.reshape(R, 1, MLA_ROPE))
        outs['ssm_s'].append(s_new.reshape(R, SSD_HEADS, SSD_HEAD_DIM, SSD_STATE))
        outs['conv_s'].append(jnp.swapaxes(cs_s, 0, 1))

    st = lambda k: jnp.stack(outs[k])

    def with_meta(k):
        meta = jnp.stack([m for m, _ in outs[k]])
        rows = jnp.stack([r for _, r in outs[k]])
        return jnp.concatenate([jnp.broadcast_to(meta, (depth, B) + meta.shape[2:]), rows], 2)

    return (hp, hs.reshape(R, 1, D), with_meta('ckv_p'), with_meta('kr_p'), st('ssm_p'), st('conv_p'),
            st('ckv_s'), st('kr_s'), st('ssm_s'), st('conv_s'))
```

```python
import functools
import math

import jax
import jax.numpy as jnp
from jax import lax
from jax.experimental import pallas as pl
from jax.experimental.pallas import tpu as pltpu

N_META = 16
NORM_EPS = 1e-6
SSD_HEADS = 16
SSD_HEAD_DIM = 64
SSD_INNER = SSD_HEADS * SSD_HEAD_DIM
SSD_GROUPS = 4
HEADS_PER_GROUP = SSD_HEADS // SSD_GROUPS
SSD_STATE = 128
SSD_CONV = 4
SSD_CHUNK = 128
SSD_BC = SSD_GROUPS * SSD_STATE
SSD_CONV_DIM = SSD_INNER + 2 * SSD_BC
MLA_HEADS = 8
MLA_NOPE = 64
MLA_ROPE = 32
MLA_V = 64
MLA_Q_LORA = 384
MLA_KV_LORA = 256
MLA_INNER = MLA_HEADS * MLA_V
MLA_SCALE = (MLA_NOPE + MLA_ROPE) ** -0.5
ROPE_BASE = 10000.0
LOG2E = math.log2(math.e)

LANES = 128
HEAD_PAD = LANES
ROPE_LANE0 = MLA_NOPE
ROPE_HALF = MLA_ROPE // 2
CTX_ROW0 = 8 - (SSD_CONV - 1)
VMEM_LIMIT = 56 * 1024 * 1024

BF16 = jnp.bfloat16
F32 = jnp.float32


def _cparams(sem):
    return pltpu.CompilerParams(dimension_semantics=sem, vmem_limit_bytes=VMEM_LIMIT)


def _full(shape):
    n = len(shape)
    return pl.BlockSpec(shape, lambda *_: (0,) * n)


def _rms(x, w):
    return x * lax.rsqrt(jnp.mean(x * x, -1, keepdims=True) + NORM_EPS) * w


def _sigmoid(x):
    return 0.5 + 0.5 * jnp.tanh(0.5 * x)


def _silu(x):
    return x * _sigmoid(x)


def _softplus(x):
    return jnp.maximum(x, 0.0) + jnp.log(1.0 + jnp.exp(-jnp.abs(x)))


def _dot(a, b):
    return jnp.dot(a, b, preferred_element_type=F32)


def _dot_nt(a, b):
    return lax.dot_general(a, b, (((1,), (1,)), ((), ())), preferred_element_type=F32)


def _dot_tn(a, b):
    return lax.dot_general(a, b, (((0,), (0,)), ((), ())), preferred_element_type=F32)


def _heads(x):
    return jnp.stack([x[:, hh * HEAD_PAD:(hh + 1) * HEAD_PAD] for hh in range(MLA_HEADS)])


def _bdot_nt(a, b):
    return lax.dot_general(a, b, (((2,), (2,)), ((0,), (0,))), preferred_element_type=F32)


def _bdot(a, b):
    return lax.dot_general(a, b, (((2,), (1,)), ((0,), (0,))), preferred_element_type=F32)


def _split3(a):
    a1 = a.astype(BF16)
    r = a - a1.astype(F32)
    a2 = r.astype(BF16)
    a3 = (r - a2.astype(F32)).astype(BF16)
    return a1, a2, a3


def _dot_sel(a, sel):
    a1, a2, a3 = _split3(a)
    return _dot(a1, sel) + _dot(a2, sel) + _dot(a3, sel)


def _dot_sel2(a, sel):
    a1 = a.astype(BF16)
    a2 = (a - a1.astype(F32)).astype(BF16)
    return _dot(a1, sel) + _dot(a2, sel)


def _sel_dot(sel, a):
    a1, a2, a3 = _split3(a)
    return _dot(sel, a1) + _dot(sel, a2) + _dot(sel, a3)


def _head_expand():
    r = lax.broadcasted_iota(jnp.int32, (SSD_HEADS, SSD_INNER), 0)
    c = lax.broadcasted_iota(jnp.int32, (SSD_HEADS, SSD_INNER), 1)
    return jnp.where(c // SSD_HEAD_DIM == r, 1.0, 0.0).astype(BF16)


def _rope_lanes(x, c, s1, s2):
    n = x.shape[-1]
    return x * c + pltpu.roll(x, n - ROPE_HALF, 1) * s1 + pltpu.roll(x, ROPE_HALF, 1) * s2


def _seq_inproj_kernel(x_ref, normw_ref, wxbc_ref, convw_ref, convb_ref, ctx0_ref,
                       wdt_ref, dtb_ref, wdtT_ref, dtbT_ref,
                       wcq_ref, qnw_ref, wuq_ref, wckv_ref, kvnw_ref, wuk_ref, wuv_ref, vone_ref, wkr_ref,
                       rc_ref, rs1_ref, rs2_ref,
                       xbc_out, dt_out, dtT_out, q_out, k_out, v_out, ckv_out, kr_out, cs_out,
                       rbuf, ybuf, ctxbuf):
    t = pl.program_id(1)
    T = x_ref.shape[1]
    NS, RP, _ = rbuf.shape
    S = RP // 8
    nctx = SSD_CONV - 1
    h = _rms(x_ref[0], normw_ref[...]).astype(BF16)

    @pl.when(t == 0)
    def _():
        ctxbuf[0:nctx, :] = ctx0_ref[...]
        rbuf[:, T:RP, :] = jnp.zeros((NS, RP - T, LANES), F32)

    sub = lax.broadcasted_iota(jnp.int32, (8, LANES), 0)
    for cs in range(NS):
        lanes = slice(cs * LANES, (cs + 1) * LANES)
        if cs % 2 == 0:
            raw2 = _dot(h, wxbc_ref[:, cs * LANES:(cs + 2) * LANES])
            rbuf[cs, 0:T, :] = raw2[:, :LANES]
            rbuf[cs + 1, 0:T, :] = raw2[:, LANES:]
        blocks = [rbuf[cs, pl.ds(v, 8, stride=S), :] for v in range(S)]
        perm = jnp.concatenate(blocks, axis=0)
        wrap = [jnp.where(sub == 0, ctxbuf[i:i + 1, lanes], pltpu.roll(blocks[S - nctx + i], 1, 0))
                for i in range(nctx)]
        acc = convb_ref[:, lanes] + perm * convw_ref[nctx:nctx + 1, lanes]
        for j in range(1, SSD_CONV):
            shifted = jnp.concatenate(wrap[nctx - j:] + [perm[:RP - 8 * j]], axis=0)
            acc = acc + shifted * convw_ref[nctx - j:nctx - j + 1, lanes]
        for v in range(S):
            ybuf[cs, pl.ds(v, 8, stride=S), :] = acc[8 * v:8 * v + 8]
        xbc_out[0, cs] = ybuf[cs, 0:T, :]
    for cs in range(NS):
        last = rbuf[cs, T - nctx:T, :]
        cs_out[0, :, cs * LANES:(cs + 1) * LANES] = last
        ctxbuf[0:nctx, cs * LANES:(cs + 1) * LANES] = last

    dt_out[0] = _softplus(_dot(h, wdt_ref[...]) + dtb_ref[...])
    dtT_out[0] = _softplus(_dot_nt(wdtT_ref[...], h) + dtbT_ref[...])

    rc, rs1, rs2 = rc_ref[...], rs1_ref[...], rs2_ref[...]
    cq = _rms(_dot(h, wcq_ref[...]), qnw_ref[...]).astype(BF16)
    q = _dot(cq, wuq_ref[...])
    q = _rope_lanes(q, jnp.tile(rc, (1, MLA_HEADS)), jnp.tile(rs1, (1, MLA_HEADS)),
                    jnp.tile(rs2, (1, MLA_HEADS)))
    q_out[0] = (q * (MLA_SCALE * LOG2E)).astype(BF16)

    kr = _rope_lanes(_dot(h, wkr_ref[...]), rc, rs1, rs2)
    kr_out[0] = kr[:, ROPE_LANE0:ROPE_LANE0 + MLA_ROPE]
    ckv = _rms(_dot(h, wckv_ref[...]), kvnw_ref[...])
    ckv_out[0] = ckv
    cb = ckv.astype(BF16)
    k_out[0] = (_dot(cb, wuk_ref[...]) + jnp.tile(kr, (1, MLA_HEADS))).astype(BF16)
    v_out[0] = (_dot(cb, wuv_ref[...]) + vone_ref[...]).astype(BF16)


def _seq_inproj(x, ctx0, rope_tabs, lw, tile):
    B, L, D = x.shape
    T = min(tile, L)
    NS = SSD_CONV_DIM // LANES
    stride = T // 8 + 4
    assert stride % 8 != 0 and T % 8 == 0
    rc, rs1, rs2 = rope_tabs
    row = lambda w: pl.BlockSpec((1, T, w), lambda b, t: (b, t, 0))
    tab = pl.BlockSpec((T, LANES), lambda b, t: (t, 0))
    weights = [lw['norm_w'], lw['w_xbc'], lw['conv_w'], lw['conv_b'], ctx0,
               lw['w_dt'], lw['dt_bias'], lw['w_dtT'], lw['dt_biasT'],
               lw['w_cq'], lw['q_norm_w'], lw['w_uq_p'], lw['w_ckv'], lw['kv_norm_w'],
               lw['w_uk_p'], lw['w_uv_p'], lw['v_ones'], lw['w_kr_p']]
    out_shape = (
        jax.ShapeDtypeStruct((B, NS, L, LANES), F32),
        jax.ShapeDtypeStruct((B, L, SSD_HEADS), F32),
        jax.ShapeDtypeStruct((B, SSD_HEADS, L), F32),
        jax.ShapeDtypeStruct((B, L, MLA_HEADS * HEAD_PAD), BF16),
        jax.ShapeDtypeStruct((B, L, MLA_HEADS * HEAD_PAD), BF16),
        jax.ShapeDtypeStruct((B, L, MLA_HEADS * HEAD_PAD), BF16),
        jax.ShapeDtypeStruct((B, L, MLA_KV_LORA), F32),
        jax.ShapeDtypeStruct((B, L, MLA_ROPE), F32),
        jax.ShapeDtypeStruct((B, SSD_CONV - 1, SSD_CONV_DIM), F32),
    )
    out_specs = (pl.BlockSpec((1, NS, T, LANES), lambda b, t: (b, 0, t, 0)), row(SSD_HEADS),
                 pl.BlockSpec((1, SSD_HEADS, T), lambda b, t: (b, 0, t)),
                 row(MLA_HEADS * HEAD_PAD), row(MLA_HEADS * HEAD_PAD), row(MLA_HEADS * HEAD_PAD),
                 row(MLA_KV_LORA), row(MLA_ROPE),
                 pl.BlockSpec((1, SSD_CONV - 1, SSD_CONV_DIM), lambda b, t: (b, 0, 0)))
    return pl.pallas_call(
        _seq_inproj_kernel,
        grid=(B, L // T),
        in_specs=[row(D)] + [_full(w.shape) for w in weights] + [tab, tab, tab],
        out_specs=out_specs,
        out_shape=out_shape,
        scratch_shapes=[pltpu.VMEM((NS, 8 * stride, LANES), F32),
                        pltpu.VMEM((NS, 8 * stride, LANES), F32),
                        pltpu.VMEM((8, SSD_CONV_DIM), F32)],
        compiler_params=_cparams(("parallel", "arbitrary")),
    )(x, *weights, rc, rs1, rs2)


def _ssd_seq_kernel(xbc_ref, dtall_ref, dtTall_ref, dtT_ref, alog_ref, alogT_ref, dskip_ref, s0_ref,
                    y_out, s_out, sT, acs_s, acsT_s, dtw_s, cdec_s):
    c = pl.program_id(1)
    NCH, Q, _ = acs_s.shape
    CPS = xbc_ref.shape[2] // Q
    GW = HEADS_PER_GROUP * SSD_HEAD_DIM
    ri = lax.broadcasted_iota(jnp.int32, (Q, Q), 0)
    ci = lax.broadcasted_iota(jnp.int32, (Q, Q), 1)
    causal = ci <= ri

    @pl.when(c == 0)
    def _():
        sT[...] = s0_ref[...]
        tril = jnp.broadcast_to(jnp.where(causal, 1.0, 0.0).astype(BF16)[None], (NCH, Q, Q))
        triu = jnp.broadcast_to(jnp.where(ri <= ci, 1.0, 0.0).astype(BF16)[None], (NCH, Q, Q))
        dt3 = dtall_ref[0].reshape(NCH, Q, SSD_HEADS)
        a1, a2, a3 = _split3(dt3 * (-jnp.exp(alog_ref[...])))
        acs3 = _bdot(tril, a1) + _bdot(tril, a2) + _bdot(tril, a3)
        aT = dtTall_ref[0] * (-jnp.exp(alogT_ref[...]))
        b1, b2, b3 = _split3(jnp.stack([aT[:, k * Q:(k + 1) * Q] for k in range(NCH)]))
        acs_s[...] = acs3
        acsT_s[...] = _bdot(b1, triu) + _bdot(b2, triu) + _bdot(b3, triu)
        end3 = acs3[:, Q - 1:Q, :]
        dtw_s[...] = dt3 * jnp.exp(end3 - acs3)
        cdec_s[...] = jnp.exp(end3)

    expand = _head_expand()
    lane = lax.broadcasted_iota(jnp.int32, (Q, LANES), 1)
    for k in range(CPS):
        rows = slice(k * Q, (k + 1) * Q)
        ch = c * CPS + k
        acs = acs_s[ch]
        acsT = acsT_s[ch]
        dtT = dtT_ref[0, :, rows]
        dtw_e = _dot_sel2(dtw_s[ch], expand)
        chunk_decay = _dot_sel(cdec_s[ch], expand)
        xs = _silu(jnp.concatenate([xbc_ref[0, cs, rows, :] for cs in range(SSD_INNER // LANES)], axis=1))
        xs_b = xs.astype(BF16)
        xw = (xs * dtw_e).astype(BF16)
        dxs = xs * dskip_ref[...]

        for g in range(SSD_GROUPS):
            lo = g * GW
            Bg = _silu(xbc_ref[0, SSD_INNER // LANES + g, rows, :]).astype(BF16)
            Cg_f = _silu(xbc_ref[0, (SSD_INNER + SSD_BC) // LANES + g, rows, :])
            cb = _dot_nt(Cg_f.astype(BF16), Bg)
            heads = range(g * HEADS_PER_GROUP, (g + 1) * HEADS_PER_GROUP)
            acs_b = jnp.stack([jnp.broadcast_to(acs[:, hh:hh + 1], (Q, LANES)) for hh in heads])
            acs_r = jnp.stack([acsT[hh:hh + 1, :] for hh in heads])
            dt_r = jnp.stack([dtT[hh:hh + 1, :] for hh in heads])
            m = cb[None] * jnp.exp(jnp.where(causal[None], acs_b[:, :, :Q] - acs_r, -jnp.inf)) * dt_r
            lhs = jnp.concatenate([Cg_f[None] * jnp.exp(acs_b), m], axis=2).astype(BF16)
            pair_rhs = [jnp.concatenate([sT[:, lo + pr * LANES:lo + (pr + 1) * LANES].astype(BF16),
                                         xs_b[:, lo + pr * LANES:lo + (pr + 1) * LANES]], axis=0)
                        for pr in range(HEADS_PER_GROUP // 2)]
            y = _bdot(lhs, jnp.stack([pair_rhs[e // 2] for e in range(HEADS_PER_GROUP)]))
            for pr in range(HEADS_PER_GROUP // 2):
                plo = lo + pr * LANES
                y_out[0, rows, plo:plo + LANES] = (jnp.where(lane < SSD_HEAD_DIM, y[2 * pr], y[2 * pr + 1])
                                                   + dxs[:, plo:plo + LANES])
            sT[:, lo:lo + GW] = sT[:, lo:lo + GW] * chunk_decay[:, lo:lo + GW] + _dot_tn(Bg, xw[:, lo:lo + GW])

    @pl.when(c == pl.num_programs(1) - 1)
    def _():
        for t in range(SSD_INNER // LANES):
            s_out[0, t * LANES:(t + 1) * LANES, :] = sT[:, t * LANES:(t + 1) * LANES].T


def _ssd_seq(xbc_c, dt, dtT, s0T, lw, chunk):
    B, NS, L, _ = xbc_c.shape
    Q = min(chunk, L)
    R = Q * min(SSD_CHUNKS_PER_STEP, L // Q)
    assert SSD_STATE == LANES
    return pl.pallas_call(
        _ssd_seq_kernel,
        grid=(B, L // R),
        in_specs=[pl.BlockSpec((1, NS, R, LANES), lambda b, c: (b, 0, c, 0)),
                  pl.BlockSpec((1, L, SSD_HEADS), lambda b, c: (b, 0, 0)),
                  pl.BlockSpec((1, SSD_HEADS, L), lambda b, c: (b, 0, 0)),
                  pl.BlockSpec((1, SSD_HEADS, R), lambda b, c: (b, 0, c)),
                  _full((1, SSD_HEADS)), _full((SSD_HEADS, 1)), _full((1, SSD_INNER)),
                  _full((SSD_STATE, SSD_INNER))],
        out_specs=(pl.BlockSpec((1, R, SSD_INNER), lambda b, c: (b, c, 0)),
                   pl.BlockSpec((1, SSD_INNER, SSD_STATE), lambda b, c: (b, 0, 0))),
        out_shape=(jax.ShapeDtypeStruct((B, L, SSD_INNER), F32),
                   jax.ShapeDtypeStruct((B, SSD_INNER, SSD_STATE), F32)),
        scratch_shapes=[pltpu.VMEM((SSD_STATE, SSD_INNER), F32),
                        pltpu.VMEM((L // Q, Q, SSD_HEADS), F32),
                        pltpu.VMEM((L // Q, SSD_HEADS, Q), F32),
                        pltpu.VMEM((L // Q, Q, SSD_HEADS), F32),
                        pltpu.VMEM((L // Q, 1, SSD_HEADS), F32)],
        compiler_params=_cparams(("parallel", "arbitrary")),
    )(xbc_c, dt, dtT, dtT, lw['a_log'], lw['a_logT'], lw['d_skip_e'], s0T)


def _attn_seq_kernel(*refs, n_pre, single_tile):
    if n_pre:
        q_ref, k_ref, v_ref, kpre_ref, vpre_ref, o_ref, m_s, acc_s = refs
    else:
        q_ref, k_ref, v_ref, o_ref, m_s, acc_s = refs
    i = pl.program_id(1)
    TQ = q_ref.shape[1]
    TKB = 2 * TQ
    causal = (lax.broadcasted_iota(jnp.int32, (TQ, TQ), 1)
              <= lax.broadcasted_iota(jnp.int32, (TQ, TQ), 0))[None]
    mshape = (MLA_HEADS, TQ, HEAD_PAD)
    m_s[...] = jnp.full(mshape, -jnp.inf, F32)
    acc_s[...] = jnp.zeros(mshape, F32)

    def lanes_of(m, n):
        return m[..., :n] if n <= HEAD_PAD else jnp.concatenate([m] * (n // HEAD_PAD), -1)

    def update(s, vs):
        m_old = m_s[...]
        m_new = jnp.maximum(m_old, jnp.max(s, -1, keepdims=True))
        p = jnp.exp2(s - lanes_of(m_new, s.shape[-1])).astype(BF16)
        acc_s[...] = acc_s[...] * jnp.exp2(m_old - m_new) + _bdot(p, vs)
        m_s[...] = m_new

    def body(j, carry):
        r = pl.multiple_of(j * TKB, TKB)
        update(_bdot_nt(_heads(q_ref[0]), _heads(k_ref[0, pl.ds(r, TKB), :])),
               _heads(v_ref[0, pl.ds(r, TKB), :]))
        return carry

    lax.fori_loop(0, lax.div(i, 2), body, 0)

    def tail(with_full):
        q3 = _heads(q_ref[0])
        ss, vv = [], []
        if n_pre:
            col = lax.broadcasted_iota(jnp.int32, (TQ, kpre_ref.shape[0]), 1)
            ss.append(jnp.where((col < n_pre)[None], _bdot_nt(q3, _heads(kpre_ref[...])), -jnp.inf))
            vv.append(_heads(vpre_ref[...]))
        r_diag = pl.multiple_of(i * TQ, TQ)
        if with_full:
            r_full = pl.multiple_of(r_diag - TQ, TQ)
            ss.append(_bdot_nt(q3, _heads(k_ref[0, pl.ds(r_full, TQ), :])))
            vv.append(_heads(v_ref[0, pl.ds(r_full, TQ), :]))
        ss.append(jnp.where(causal, _bdot_nt(q3, _heads(k_ref[0, pl.ds(r_diag, TQ), :])), -jnp.inf))
        vv.append(_heads(v_ref[0, pl.ds(r_diag, TQ), :]))
        update(jnp.concatenate(ss, -1), jnp.concatenate(vv, 1))

    if single_tile:
        tail(False)
    else:
        pl.when(lax.rem(i, 2) == 0)(lambda: tail(False))
        pl.when(lax.rem(i, 2) == 1)(lambda: tail(True))

    lane = lax.broadcasted_iota(jnp.int32, (TQ, HEAD_PAD), 1)
    for hp in range(MLA_HEADS // 2):
        a0, a1 = acc_s[2 * hp], acc_s[2 * hp + 1]
        o0 = a0 * (1.0 / a0[:, MLA_V:MLA_V + 1])
        o1 = a1 * (1.0 / a1[:, 0:1])
        o_ref[0, :, hp * HEAD_PAD:(hp + 1) * HEAD_PAD] = jnp.where(lane < MLA_V, o0, o1)


def _attn_seq(q, k, v, kpre, vpre, tile):
    B, L, W = q.shape
    TQ = min(tile, L)
    n_pre = 0 if kpre is None else kpre.shape[0]
    seq = pl.BlockSpec((1, L, W), lambda b, i: (b, 0, 0))
    in_specs = [pl.BlockSpec((1, TQ, W), lambda b, i: (b, i, 0)), seq, seq]
    args = [q, k, v]
    if n_pre:
        pad = ((0, HEAD_PAD - n_pre), (0, 0))
        kpre, vpre = jnp.pad(kpre, pad), jnp.pad(vpre, pad)
        in_specs += [_full(kpre.shape), _full(vpre.shape)]
        args += [kpre, vpre]
    return pl.pallas_call(
        functools.partial(_attn_seq_kernel, n_pre=n_pre, single_tile=(L == TQ)),
        grid=(B, L // TQ),
        in_specs=in_specs,
        out_specs=pl.BlockSpec((1, TQ, MLA_INNER), lambda b, i: (b, i, 0)),
        out_shape=jax.ShapeDtypeStruct((B, L, MLA_INNER), F32),
        scratch_shapes=[pltpu.VMEM((MLA_HEADS, TQ, HEAD_PAD), F32),
                        pltpu.VMEM((MLA_HEADS, TQ, HEAD_PAD), F32)],
        compiler_params=_cparams(("parallel", "arbitrary")),
    )(*args)


def _merge_kernel(x_ref, yssd_ref, ymla_ref, normw_ref, wz_ref, wg_ref, wgate_ref, ssdnw_ref,
                  wps_ref, wpm_ref, wo_ref, fnw_ref, o_ref, *, final):
    x = x_ref[...]
    h = _rms(x, normw_ref[...]).astype(BF16)
    z = _dot(h, wz_ref[...])
    y_ssd = _rms(yssd_ref[...] * _silu(z), ssdnw_ref[...]).astype(BF16)
    y_mla = (ymla_ref[...] * _silu(_dot(h, wg_ref[...]))).astype(BF16)
    gate = _sigmoid(_dot(h, wgate_ref[...]))
    D = x.shape[-1]
    u = gate[:, :D] * _dot(y_ssd, wps_ref[...]) + gate[:, D:] * _dot(y_mla, wpm_ref[...])
    out = x + _dot(u.astype(BF16), wo_ref[...])
    if final:
        out = _rms(out, fnw_ref[...])
    o_ref[...] = out


def _merge_rows(x, y_ssd, y_mla, lw, fnw, final, tile):
    R, D = x.shape
    T = min(tile, R)
    row = lambda w: pl.BlockSpec((T, w), lambda r: (r, 0))
    weights = [lw['norm_w'], lw['w_z'], lw['w_g'], lw['w_gate'], lw['ssd_norm_w'],
               lw['w_proj_ssd'], lw['w_proj_mla'], lw['w_out'], fnw]
    return pl.pallas_call(
        functools.partial(_merge_kernel, final=final),
        grid=(R // T,),
        in_specs=[row(D), row(SSD_INNER), row(MLA_INNER)] + [_full(w.shape) for w in weights],
        out_specs=row(D),
        out_shape=jax.ShapeDtypeStruct((R, D), F32),
        compiler_params=_cparams(("parallel",)),
    )(x, y_ssd, y_mla, *weights)


def _sample_inproj_kernel(x_ref, normw_ref, wxbc_ref, convw_ref, convb_ref, ctx_ref,
                          wdt_ref, dtb_ref, wcq_ref, qnw_ref, wuqn_ref, wuqr_ref, wukT_ref,
                          wckv_ref, kvnw_ref, wkr_ref, rc_ref, rs1_ref, rs2_ref,
                          xbc_out, dt_out, cs_out, qlat_out, qr_out, ckv_out, kr_out):
    C = SSD_CONV_DIM
    h = _rms(x_ref[...], normw_ref[...]).astype(BF16)
    raw = _dot(h, wxbc_ref[...])
    acc = convb_ref[...] + raw * convw_ref[SSD_CONV - 1:SSD_CONV, :]
    for k in range(SSD_CONV - 1):
        acc = acc + ctx_ref[k] * convw_ref[k:k + 1, :]
    xbc_out[...] = _silu(acc)
    for k in range(SSD_CONV - 2):
        cs_out[k] = ctx_ref[k + 1]
    cs_out[SSD_CONV - 2] = raw
    dt_out[...] = _softplus(_dot(h, wdt_ref[...]) + dtb_ref[...])

    rc, rs1, rs2 = rc_ref[...], rs1_ref[...], rs2_ref[...]
    cq = _rms(_dot(h, wcq_ref[...]), qnw_ref[...]).astype(BF16)
    qscale = MLA_SCALE * LOG2E
    qn = (_dot(cq, wuqn_ref[...]) * qscale).astype(BF16)
    qr = _rope_lanes(_dot(cq, wuqr_ref[...]), jnp.tile(rc, (1, MLA_HEADS)),
                     jnp.tile(rs1, (1, MLA_HEADS)), jnp.tile(rs2, (1, MLA_HEADS))) * qscale
    for hh in range(MLA_HEADS):
        qlat_out[:, hh, :] = _dot(qn[:, hh * MLA_NOPE:(hh + 1) * MLA_NOPE], wukT_ref[hh])
        qr_out[:, hh, :] = qr[:, hh * LANES + ROPE_LANE0:hh * LANES + ROPE_LANE0 + MLA_ROPE]
    kr = _rope_lanes(_dot(h, wkr_ref[...]), rc, rs1, rs2)
    kr_out[...] = kr[:, ROPE_LANE0:ROPE_LANE0 + MLA_ROPE]
    ckv_out[...] = _rms(_dot(h, wckv_ref[...]), kvnw_ref[...])


def _sample_inproj(x, ctx, rope_tabs, lw):
    R, D = x.shape
    rc, rs1, rs2 = rope_tabs
    args = [x, lw['norm_w'], lw['w_xbc'], lw['conv_w'], lw['conv_b'], ctx, lw['w_dt'], lw['dt_bias'],
            lw['w_cq'], lw['q_norm_w'], lw['w_uq_nope'], lw['w_uq_rope_p'], lw['w_ukT'],
            lw['w_ckv'], lw['kv_norm_w'], lw['w_kr_p'], rc, rs1, rs2]
    out_shape = (jax.ShapeDtypeStruct((R, SSD_CONV_DIM), F32),
                 jax.ShapeDtypeStruct((R, SSD_HEADS), F32),
                 jax.ShapeDtypeStruct((SSD_CONV - 1, R, SSD_CONV_DIM), F32),
                 jax.ShapeDtypeStruct((R, MLA_HEADS, MLA_KV_LORA), F32),
                 jax.ShapeDtypeStruct((R, MLA_HEADS, MLA_ROPE), F32),
                 jax.ShapeDtypeStruct((R, MLA_KV_LORA), F32),
                 jax.ShapeDtypeStruct((R, MLA_ROPE), F32))
    return pl.pallas_call(
        _sample_inproj_kernel,
        grid=(1,),
        in_specs=[_full(a.shape) for a in args],
        out_specs=tuple(_full(s.shape) for s in out_shape),
        out_shape=out_shape,
        compiler_params=_cparams(("arbitrary",)),
    )(*args)


def _ssd_step_kernel(xbc_ref, dt_ref, alog_ref, dskip_ref, s_ref, y_out, s_out):
    NB = xbc_ref.shape[0]
    GW = HEADS_PER_GROUP * SSD_HEAD_DIM
    expand = _head_expand()
    dt_e = _dot_sel(dt_ref[...], expand)
    a_e = _dot_sel(-jnp.exp(alog_ref[...]), expand)
    xs = xbc_ref[:, 0:SSD_INNER]
    xdt = xs * dt_e
    dec = jnp.exp(dt_e * a_e)
    dxs = xs * dskip_ref[...]
    KR = 16
    row = lax.broadcasted_iota(jnp.int32, (KR, SSD_INNER), 0)
    col = lax.broadcasted_iota(jnp.int32, (KR, SSD_INNER), 1)
    gmask = jnp.where((row < 2 * SSD_GROUPS) & (col // GW == row % SSD_GROUPS), 1.0, 0.0)
    rown = lax.broadcasted_iota(jnp.int32, (KR, SSD_STATE), 0)
    dec_rows = jnp.where((rown >= 2 * SSD_GROUPS) & (rown < 2 * SSD_GROUPS + 3), 1.0, 0.0)
    for b in range(NB):
        xb = xdt[b:b + 1, :]
        x1 = xb.astype(BF16).astype(F32)
        d1, d2, d3 = [d.astype(F32) for d in _split3(dec[b:b + 1, :])]
        lhs = jnp.where(row < SSD_GROUPS, x1, xb - x1) * gmask
        lhs = lhs + jnp.where(row == 2 * SSD_GROUPS, d1,
                              jnp.where(row == 2 * SSD_GROUPS + 1, d2,
                                        jnp.where(row == 2 * SSD_GROUPS + 2, d3, 0.0)))
        bmat = jnp.zeros((KR, SSD_STATE), F32)
        cmat = jnp.zeros((KR, SSD_STATE), F32)
        for g in range(SSD_GROUPS):
            bg = xbc_ref[b:b + 1, SSD_INNER + g * SSD_STATE:SSD_INNER + (g + 1) * SSD_STATE]
            cg = xbc_ref[b:b + 1, SSD_INNER + SSD_BC + g * SSD_STATE:SSD_INNER + SSD_BC + (g + 1) * SSD_STATE]
            bmat = bmat + jnp.where((rown < 2 * SSD_GROUPS) & (rown % SSD_GROUPS == g), bg, 0.0)
            cmat = cmat + jnp.where(rown == g, cg, 0.0)
        rhs = jnp.concatenate([bmat, dec_rows], axis=1).astype(BF16)
        r = _dot_tn(lhs.astype(BF16), rhs)
        s_new = s_ref[b] * r[:, SSD_STATE:] + r[:, :SSD_STATE]
        s_out[b] = s_new
        yt = _dot_nt(cmat.astype(BF16), s_new.astype(BF16))
        y_out[b:b + 1, :] = jnp.sum(yt * gmask, axis=0, keepdims=True) + dxs[b:b + 1, :]


def _ssd_step(xbc_c, dt, state, layer, lw, nb):
    R = xbc_c.shape[0]
    return pl.pallas_call(
        _ssd_step_kernel,
        grid=(R // nb,),
        in_specs=[pl.BlockSpec((nb, SSD_CONV_DIM), lambda i: (i, 0)),
                  pl.BlockSpec((nb, SSD_HEADS), lambda i: (i, 0)),
                  _full((1, SSD_HEADS)), _full((1, SSD_INNER)),
                  pl.BlockSpec((None, nb, SSD_INNER, SSD_STATE), lambda i: (layer, i, 0, 0))],
        out_specs=(pl.BlockSpec((nb, SSD_INNER), lambda i: (i, 0)),
                   pl.BlockSpec((nb, SSD_INNER, SSD_STATE), lambda i: (i, 0, 0))),
        out_shape=(jax.ShapeDtypeStruct((R, SSD_INNER), F32),
                   jax.ShapeDtypeStruct((R, SSD_INNER, SSD_STATE), F32)),
        compiler_params=_cparams(("parallel",)),
    )(xbc_c, dt, lw['a_log'], lw['d_skip_e'], state)


def _decode_kernel(pt_ref, qlat_ref, qr_ref, ckvn_ref, krn_ref, cckv_hbm, ckr_hbm, o_ref,
                   ckv_buf, kr_buf, sems, m_s, l_s, acc_s, *, layer, G):
    b = pl.program_id(0)
    c = pl.program_id(1)
    NB = pl.num_programs(0)
    NC = pl.num_programs(1)
    step = b * NC + c
    n_slots = ckv_buf.shape[0]
    ahead = n_slots - 1
    slot = lax.rem(step, n_slots)

    P = cckv_hbm.shape[2]

    def start_page(chunk, g):
        sl = lax.rem(chunk, n_slots)
        page = pt_ref[lax.div(chunk, NC), lax.rem(chunk, NC) * G + g]
        r = pl.multiple_of(g * P, P)
        pltpu.make_async_copy(cckv_hbm.at[layer, page], ckv_buf.at[sl, pl.ds(r, P)], sems.at[sl, 0]).start()
        pltpu.make_async_copy(ckr_hbm.at[layer, page], kr_buf.at[sl, :, pl.ds(r, P)],
                              sems.at[sl, 1]).start(priority=1)

    @pl.when(step == 0)
    def _():
        def prime(i, carry):
            @pl.when(lax.div(i, G) < NB * NC)
            def _():
                start_page(lax.div(i, G), lax.rem(i, G))
            return carry
        lax.fori_loop(0, ahead * G, prime, 0)

    pltpu.make_async_copy(ckv_buf.at[slot], ckv_buf.at[slot], sems.at[slot, 0]).wait()
    pltpu.make_async_copy(kr_buf.at[slot], kr_buf.at[slot], sems.at[slot, 1]).wait()

    @pl.when(step + ahead < NB * NC)
    def _():
        for g in range(G):
            start_page(step + ahead, g)

    @pl.when(c == 0)
    def _():
        m_s[...] = jnp.full(m_s.shape, -jnp.inf, F32)
        l_s[...] = jnp.zeros(l_s.shape, F32)
        acc_s[...] = jnp.zeros(acc_s.shape, F32)

    ql = qlat_ref[0].astype(BF16)
    qr = qr_ref[0].astype(BF16)
    kc = ckv_buf[slot].astype(BF16)
    s = _dot_nt(ql, kc) + _dot(qr, kr_buf[slot].astype(BF16))
    m, l, acc = m_s[...], l_s[...], acc_s[...]
    n_sub = 2 if G % 2 == 0 else 1
    W = G * P // n_sub
    for u in range(n_sub):
        su = s[:, u * W:(u + 1) * W]
        m_new = jnp.maximum(m, jnp.max(su, -1, keepdims=True))
        corr = jnp.exp2(m - m_new)
        p = jnp.exp2(su - m_new)
        l = l * corr + jnp.sum(p, -1, keepdims=True)
        acc = acc * corr + _dot(p.astype(BF16), kc[u * W:(u + 1) * W])
        m = m_new
    m_s[...], l_s[...], acc_s[...] = m, l, acc

    @pl.when(c == NC - 1)
    def _():
        kn = ckvn_ref[0]
        s1 = (jnp.sum(qlat_ref[0] * kn, -1, keepdims=True)
              + jnp.sum(qr_ref[0] * krn_ref[0], -1, keepdims=True))
        m2 = jnp.maximum(m, s1)
        corr2 = jnp.exp2(m - m2)
        p1 = jnp.exp2(s1 - m2)
        o_ref[0] = (acc * corr2 + p1 * kn) * (1.0 / (l * corr2 + p1))


def _decode_attn(page_table, qlat, qr, ckv_new, kr_new, cache_ckv, cache_kr, layer, G):
    R, n_pages = page_table.shape
    page = cache_ckv.shape[2]
    G = min(G, n_pages)
    grid_spec = pltpu.PrefetchScalarGridSpec(
        num_scalar_prefetch=1,
        grid=(R, n_pages // G),
        in_specs=[pl.BlockSpec((1, MLA_HEADS, MLA_KV_LORA), lambda b, c, pt: (b, 0, 0)),
                  pl.BlockSpec((1, MLA_HEADS, MLA_ROPE), lambda b, c, pt: (b, 0, 0)),
                  pl.BlockSpec((1, 1, MLA_KV_LORA), lambda b, c, pt: (b, 0, 0)),
                  pl.BlockSpec((1, 1, MLA_ROPE), lambda b, c, pt: (b, 0, 0)),
                  pl.BlockSpec(memory_space=pl.ANY),
                  pl.BlockSpec(memory_space=pl.ANY)],
        out_specs=pl.BlockSpec((1, MLA_HEADS, MLA_KV_LORA), lambda b, c, pt: (b, 0, 0)),
        scratch_shapes=[pltpu.VMEM((DECODE_SLOTS, G * page, MLA_KV_LORA), F32),
                        pltpu.VMEM((DECODE_SLOTS, MLA_ROPE, G * page), F32),
                        pltpu.SemaphoreType.DMA((DECODE_SLOTS, 2)),
                        pltpu.VMEM((MLA_HEADS, 1), F32),
                        pltpu.VMEM((MLA_HEADS, 1), F32),
                        pltpu.VMEM((MLA_HEADS, MLA_KV_LORA), F32)])
    return pl.pallas_call(
        functools.partial(_decode_kernel, layer=layer, G=G),
        grid_spec=grid_spec,
        out_shape=jax.ShapeDtypeStruct((R, MLA_HEADS, MLA_KV_LORA), F32),
        compiler_params=_cparams(("arbitrary", "arbitrary")),
    )(page_table, qlat, qr, ckv_new.reshape(R, 1, MLA_KV_LORA), kr_new.reshape(R, 1, MLA_ROPE),
      cache_ckv, jnp.swapaxes(cache_kr, 2, 3))


def _uv_proj_kernel(o_ref, wuv_ref, y_ref):
    for hh in range(MLA_HEADS):
        y_ref[:, hh * MLA_V:(hh + 1) * MLA_V] = _dot(o_ref[:, hh, :].astype(BF16), wuv_ref[hh])


def _uv_proj(o_lat, w_uv_h):
    R = o_lat.shape[0]
    return pl.pallas_call(
        _uv_proj_kernel,
        grid=(1,),
        in_specs=[_full(o_lat.shape), _full(w_uv_h.shape)],
        out_specs=_full((R, MLA_INNER)),
        out_shape=jax.ShapeDtypeStruct((R, MLA_INNER), F32),
        compiler_params=_cparams(("arbitrary",)),
    )(o_lat, w_uv_h)


def _rope_tables(pos, lane0):
    inv = 1.0 / (ROPE_BASE ** (jnp.arange(ROPE_HALF, dtype=F32) / ROPE_HALF))
    ang = pos.astype(F32)[:, None] * inv[None, :]
    cos, sin = jnp.cos(ang), jnp.sin(ang)
    n = pos.shape[0]
    z = lambda w: jnp.zeros((n, w), F32)
    pre = jnp.ones((n, lane0), F32)
    tail = LANES - lane0 - MLA_ROPE
    rc = jnp.concatenate([pre, cos, cos, z(tail)], 1)
    rs1 = jnp.concatenate([z(lane0), -sin, z(ROPE_HALF), z(tail)], 1)
    rs2 = jnp.concatenate([z(lane0), z(ROPE_HALF), sin, z(tail)], 1)
    return rc, rs1, rs2


def _layer_weights(p, l):
    D = p['w_in'].shape[1]
    w_in = p['w_in'][l]
    offs = [0]
    for n in (SSD_INNER, SSD_CONV_DIM, SSD_HEADS, MLA_Q_LORA, MLA_KV_LORA, MLA_ROPE, MLA_INNER, 2 * D):
        offs.append(offs[-1] + n)
    w_z, w_xbc, w_dt, w_cq, w_ckv, w_kr, w_g, w_gate = [w_in[:, offs[i]:offs[i + 1]] for i in range(8)]
    bf = lambda a: a.astype(BF16)
    pad_head = lambda a: jnp.pad(a, ((0, 0), (0, 0), (0, HEAD_PAD - a.shape[-1])))
    w_uq = p['w_uq'][l].reshape(MLA_Q_LORA, MLA_HEADS, MLA_NOPE + MLA_ROPE)
    w_uk = p['w_uk'][l].reshape(MLA_KV_LORA, MLA_HEADS, MLA_NOPE)
    w_uv = p['w_uv'][l]
    w_uv_pairs = w_uv.reshape(MLA_KV_LORA, MLA_HEADS // 2, 2, MLA_V)
    zv = jnp.zeros((MLA_KV_LORA, MLA_HEADS // 2, MLA_V), F32)
    w_uv_p = jnp.stack([jnp.concatenate([w_uv_pairs[:, :, 0], zv], -1),
                        jnp.concatenate([zv, w_uv_pairs[:, :, 1]], -1)], 2)
    one_at = lambda lane: (jnp.arange(HEAD_PAD) == lane).astype(F32)
    v_ones = jnp.tile(jnp.concatenate([one_at(MLA_V), one_at(0)]), MLA_HEADS // 2)[None]
    w_kr_p = jnp.pad(w_kr, ((0, 0), (ROPE_LANE0, LANES - ROPE_LANE0 - MLA_ROPE)))
    w_uq_rope_p = jnp.pad(w_uq[:, :, MLA_NOPE:], ((0, 0), (0, 0), (ROPE_LANE0, LANES - ROPE_LANE0 - MLA_ROPE)))
    return {
        'norm_w': p['norm_w'][l][None], 'w_xbc': bf(w_xbc), 'conv_w': p['conv_w'][l],
        'conv_b': p['conv_b'][l][None], 'w_dt': bf(w_dt), 'dt_bias': p['dt_bias'][l][None],
        'w_dtT': bf(w_dt.T), 'dt_biasT': p['dt_bias'][l][:, None],
        'w_cq': bf(w_cq), 'q_norm_w': p['q_norm_w'][l][None],
        'w_uq_p': bf(pad_head(w_uq).reshape(MLA_Q_LORA, MLA_HEADS * HEAD_PAD)),
        'w_uq_nope': bf(w_uq[:, :, :MLA_NOPE].reshape(MLA_Q_LORA, MLA_HEADS * MLA_NOPE)),
        'w_uq_rope_p': bf(w_uq_rope_p.reshape(MLA_Q_LORA, MLA_HEADS * LANES)),
        'w_ckv': bf(w_ckv), 'kv_norm_w': p['kv_norm_w'][l][None],
        'w_uk_p': bf(pad_head(w_uk).reshape(MLA_KV_LORA, MLA_HEADS * HEAD_PAD)),
        'w_ukT': bf(jnp.transpose(w_uk, (1, 2, 0))),
        'w_uv_p': bf(w_uv_p.reshape(MLA_KV_LORA, MLA_HEADS * HEAD_PAD)), 'v_ones': v_ones,
        'w_uv_h': bf(jnp.transpose(w_uv.reshape(MLA_KV_LORA, MLA_HEADS, MLA_V), (1, 0, 2))),
        'w_kr_p': bf(w_kr_p),
        'a_log': p['a_log'][l][None], 'a_logT': p['a_log'][l][:, None],
        'd_skip_e': jnp.repeat(p['d_skip'][l], SSD_HEAD_DIM)[None],
        'w_z': bf(w_z), 'w_g': bf(w_g), 'w_gate': bf(w_gate),
        'ssd_norm_w': p['ssd_norm_w'][l][None],
        'w_proj_ssd': bf(p['w_proj_ssd'][l]), 'w_proj_mla': bf(p['w_proj_mla'][l]), 'w_out': bf(p['w_out'][l]),
    }


SEQ_TILE = 512
ATTN_TILE = 256
MERGE_TILE = 512
STEP_ROWS = 8
SSD_CHUNKS_PER_STEP = 2
PAGES_PER_STEP = 32
DECODE_SLOTS = 3


def _seq_layer(x, ctx0, s0T, kpre, vpre, rope_tabs, lw, fnw, final, chunk, need_out=True):
    B, L, D = x.shape
    xbc_c, dt, dtT, q, k, v, ckv, kr, conv_state = _seq_inproj(x, ctx0, rope_tabs, lw, SEQ_TILE)
    y_ssd, sT = _ssd_seq(xbc_c, dt, dtT, s0T, lw, chunk)
    if not need_out:
        return None, (ckv, kr, sT, conv_state, k, v)
    y_mla = _attn_seq(q, k, v, kpre, vpre, ATTN_TILE)
    out = _merge_rows(x.reshape(B * L, D), y_ssd.reshape(B * L, SSD_INNER), y_mla.reshape(B * L, MLA_INNER),
                      lw, fnw, final, MERGE_TILE).reshape(B, L, D)
    return out, (ckv, kr, sT, conv_state, k, v)


def kernel(x_prompt, x_sample, cache_ckv, cache_kr, state_ssm, state_conv, page_table, meta_tokens,
           norm_w, w_in, conv_w, conv_b, dt_bias, a_log, d_skip, ssd_norm_w, q_norm_w, w_uq, kv_norm_w,
           w_uk, w_uv, w_proj_ssd, w_proj_mla, w_out, final_norm_w):
    p = {'norm_w': norm_w, 'w_in': w_in, 'conv_w': conv_w, 'conv_b': conv_b, 'dt_bias': dt_bias,
         'a_log': a_log, 'd_skip': d_skip, 'ssd_norm_w': ssd_norm_w, 'q_norm_w': q_norm_w, 'w_uq': w_uq,
         'kv_norm_w': kv_norm_w, 'w_uk': w_uk, 'w_uv': w_uv, 'w_proj_ssd': w_proj_ssd,
         'w_proj_mla': w_proj_mla, 'w_out': w_out}
    depth = w_in.shape[0]
    B, L, D = x_prompt.shape
    R = x_sample.shape[0]
    n_meta = meta_tokens.shape[0]
    past_len = page_table.shape[1] * cache_ckv.shape[2]
    fnw = final_norm_w[None]

    tabs_meta = _rope_tables(jnp.arange(n_meta), ROPE_LANE0)
    tabs_prompt = _rope_tables(n_meta + jnp.arange(L), ROPE_LANE0)
    tabs_sample = _rope_tables(past_len + jnp.arange(1), ROPE_LANE0)

    hm = meta_tokens[None].astype(x_prompt.dtype)
    hp = x_prompt
    hs = x_sample.reshape(R, D)
    outs = {k: [] for k in ('ckv_p', 'kr_p', 'ssm_p', 'conv_p', 'ckv_s', 'kr_s', 'ssm_s', 'conv_s')}
    for l in range(depth):
        lw = _layer_weights(p, l)
        final = l == depth - 1
        zero_ctx = jnp.zeros((SSD_CONV - 1, SSD_CONV_DIM), F32)
        zero_state = jnp.zeros((SSD_STATE, SSD_INNER), F32)
        hm_next, (ckv_m, kr_m, s_m, cs_m, k_m, v_m) = _seq_layer(
            hm, zero_ctx, zero_state, None, None, tabs_meta, lw, fnw, False, n_meta, need_out=not final)
        hp, (ckv_pp, kr_pp, s_p, cs_p, _, _) = _seq_layer(
            hp, cs_m[0], s_m[0].T, k_m[0], v_m[0], tabs_prompt, lw, fnw, final, SSD_CHUNK)
        hm = hm_next
        outs['ckv_p'].append((ckv_m, ckv_pp))
        outs['kr_p'].append((kr_m, kr_pp))
        outs['ssm_p'].append(s_p.reshape(B, SSD_HEADS, SSD_HEAD_DIM, SSD_STATE))
        outs['conv_p'].append(cs_p)

        ctx = jnp.swapaxes(state_conv[l], 0, 1)
        xbc_c, dt, cs_s, qlat, qr, ckv_s, kr_s = _sample_inproj(hs, ctx, tabs_sample, lw)
        y_ssd, s_new = _ssd_step(xbc_c, dt, state_ssm.reshape(depth, R, SSD_INNER, SSD_STATE), l, lw, STEP_ROWS)
        o_lat = _decode_attn(page_table, qlat, qr, ckv_s, kr_s, cache_ckv, cache_kr, l, PAGES_PER_STEP)
        y_mla = _uv_proj(o_lat, lw['w_uv_h'])
        hs = _merge_rows(hs, y_ssd, y_mla, lw, fnw, final, MERGE_TILE)
        outs['ckv_s'].append(ckv_s.reshape(R, 1, MLA_KV_LORA))
        outs['kr_s'].append(kr_s.reshape(R, 1, MLA_ROPE))
        outs['ssm_s'].append(s_new.reshape(R, SSD_HEADS, SSD_HEAD_DIM, SSD_STATE))
        outs['conv_s'].append(jnp.swapaxes(cs_s, 0, 1))

    st = lambda k: jnp.stack(outs[k])

    def with_meta(k):
        meta = jnp.stack([m for m, _ in outs[k]])
        rows = jnp.stack([r for _, r in outs[k]])
        return jnp.concatenate([jnp.broadcast_to(meta, (depth, B) + meta.shape[2:]), rows], 2)

    return (hp, hs.reshape(R, 1, D), with_meta('ckv_p'), with_meta('kr_p'), st('ssm_p'), st('conv_p'),
            st('ckv_s'), st('kr_s'), st('ssm_s'), st('conv_s'))
```
